```python
import jax, jax.numpy as jnp
from jax import lax
import numpy as np

D_MODEL = 2048
BATCH = 8
SEQ = 2048
DEPTH = 1

CHUNK = 64
N_MEM = 256
MIX_WIDTH = D_MODEL
W_A = MIX_WIDTH // 2
G_A = 8
GA_DIM = W_A // G_A
SGU_BLOCK = 128
W_B = MIX_WIDTH - W_A
H_B = 8
DH_B = W_B // H_B
Q_BLOCK = 128
X_HEADS = 4
X_DH = D_MODEL // X_HEADS
D_FF = 5632
EPS = 1e-6

kernel_name = "hybrid_sgu_stickbreak_macaron_block"


def rmsnorm(x, g):
    xf = x.astype(jnp.float32)
    y = xf * lax.rsqrt(jnp.mean(xf * xf, axis=-1, keepdims=True) + EPS)
    return (y * g.astype(jnp.float32)).astype(x.dtype)


def layernorm(x, g, b):
    xf = x.astype(jnp.float32)
    mu = jnp.mean(xf, axis=-1, keepdims=True)
    var = jnp.mean(jnp.square(xf - mu), axis=-1, keepdims=True)
    y = (xf - mu) * lax.rsqrt(var + EPS)
    return (y * g.astype(jnp.float32) + b.astype(jnp.float32)).astype(x.dtype)


def swiglu(x, w_in, w_out):
    gate, up = jnp.split(x @ w_in, 2, axis=-1)
    return (jax.nn.silu(gate) * up) @ w_out


def spatial_gating_unit(za, ln_g, ln_b, w_s, b_s):
    bsz, seq, _ = za.shape
    u, v = jnp.split(za, 2, axis=-1)
    v = layernorm(v.reshape(bsz, seq, G_A, GA_DIM),
                  ln_g.reshape(G_A, GA_DIM), ln_b.reshape(G_A, GA_DIM))
    v = v.reshape(bsz, seq // SGU_BLOCK, SGU_BLOCK, G_A, GA_DIM)
    cidx = jnp.arange(SGU_BLOCK) // CHUNK
    mask = cidx[None, :] <= cidx[:, None]
    w = jnp.where(mask[None], w_s, jnp.zeros((), w_s.dtype))
    mixed = jnp.einsum('gts,bnsgc->bntgc', w, v) + b_s.T[:, :, None]
    return u * mixed.reshape(bsz, seq, W_A)


def stick_breaking_attention(q, k, v):
    seq = q.shape[1]
    q = q * (DH_B ** -0.5)
    outs = []
    for i in range(seq // Q_BLOCK):
        end = (i + 1) * Q_BLOCK
        q_blk = q[:, i * Q_BLOCK:end]
        k_pre, v_pre = k[:, :end], v[:, :end]
        z = jnp.einsum('bqhd,bkhd->bhqk', q_blk, k_pre).astype(jnp.float32)
        t_pos = i * Q_BLOCK + jnp.arange(Q_BLOCK)
        s_pos = jnp.arange(end)
        causal = s_pos[None, :] < t_pos[:, None]
        log_beta = jax.nn.log_sigmoid(z)
        log_1m = jnp.where(causal, jax.nn.log_sigmoid(-z), 0.0)
        rest = lax.cumsum(log_1m, axis=3, reverse=True) - log_1m
        a = jnp.where(causal, jnp.exp(log_beta + rest), 0.0)
        outs.append(jnp.einsum('bhqk,bkhd->bqhd', a.astype(v.dtype), v_pre))
    return jnp.concatenate(outs, axis=1)


def memory_cross_attention(xn, memn, w_cq, w_ckv, w_co):
    bsz, seq, _ = xn.shape
    q = (xn @ w_cq).reshape(bsz, seq, X_HEADS, X_DH) * (X_DH ** -0.5)
    k, v = jnp.split(memn @ w_ckv, 2, axis=-1)
    k = k.reshape(bsz, N_MEM, X_HEADS, X_DH)
    v = v.reshape(bsz, N_MEM, X_HEADS, X_DH)
    s = jnp.einsum('bshd,bmhd->bhsm', q, k).astype(jnp.float32)
    p = jax.nn.softmax(s, axis=-1).astype(v.dtype)
    o = jnp.einsum('bhsm,bmhd->bshd', p, v).reshape(bsz, seq, D_MODEL)
    return o @ w_co


def setup_inputs(seed: int = 0) -> dict:
    key = jax.random.key(seed)
    ks = jax.random.split(key, 32)
    f32 = jnp.float32

    def w(k, shape, fan_in):
        return jax.random.normal(k, shape, f32) * (fan_in ** -0.5)

    def gain(k, shape):
        return 1.0 + 0.05 * jax.random.normal(k, shape, f32)

    L = DEPTH
    return {
        "x": jax.random.normal(ks[0], (BATCH, SEQ, D_MODEL), f32),
        "mem": jax.random.normal(ks[1], (BATCH, N_MEM, D_MODEL), f32),
        "ffn1_norm": gain(ks[2], (L, D_MODEL)),
        "ffn1_w_in": w(ks[3], (L, D_MODEL, 2 * D_FF), D_MODEL),
        "ffn1_w_out": w(ks[4], (L, D_FF, D_MODEL), D_FF),
        "mix_norm": gain(ks[5], (L, D_MODEL)),
        "w_mix_in": w(ks[6], (L, D_MODEL, 2 * W_A + 3 * W_B), D_MODEL),
        "ln_v_gain": gain(ks[7], (L, W_A)),
        "ln_v_bias": 0.02 * jax.random.normal(ks[8], (L, W_A), f32),
        "spatial_w": w(ks[9], (L, G_A, SGU_BLOCK, SGU_BLOCK), SGU_BLOCK),
        "spatial_b": 1.0 + 0.1 * jax.random.normal(ks[10], (L, G_A, SGU_BLOCK), f32),
        "gnorm_a": gain(ks[11], (L, W_A)),
        "gnorm_b": gain(ks[12], (L, W_B)),
        "w_mix_out": w(ks[13], (L, MIX_WIDTH, D_MODEL), MIX_WIDTH),
        "cross_norm": gain(ks[14], (L, D_MODEL)),
        "mem_norm": gain(ks[15], (L, D_MODEL)),
        "w_cq": w(ks[16], (L, D_MODEL, D_MODEL), D_MODEL),
        "w_ckv": w(ks[17], (L, D_MODEL, 2 * D_MODEL), D_MODEL),
        "w_co": w(ks[18], (L, D_MODEL, D_MODEL), D_MODEL),
        "ffn2_norm": gain(ks[19], (L, D_MODEL)),
        "ffn2_w_in": w(ks[20], (L, D_MODEL, 2 * D_FF), D_MODEL),
        "ffn2_w_out": w(ks[21], (L, D_FF, D_MODEL), D_FF),
        "final_norm": gain(ks[22], (D_MODEL,)),
    }


def reference(x, mem, ffn1_norm, ffn1_w_in, ffn1_w_out, mix_norm, w_mix_in,
              ln_v_gain, ln_v_bias, spatial_w, spatial_b, gnorm_a, gnorm_b,
              w_mix_out, cross_norm, mem_norm, w_cq, w_ckv, w_co,
              ffn2_norm, ffn2_w_in, ffn2_w_out, final_norm):
    bsz, seq, _ = x.shape
    h = x
    for l in range(DEPTH):
        h = h + 0.5 * swiglu(rmsnorm(h, ffn1_norm[l]), ffn1_w_in[l], ffn1_w_out[l])

        z = rmsnorm(h, mix_norm[l]) @ w_mix_in[l]
        za = jax.nn.gelu(z[..., :2 * W_A])
        q, k, v = jnp.split(z[..., 2 * W_A:], 3, axis=-1)
        y_a = spatial_gating_unit(za, ln_v_gain[l], ln_v_bias[l], spatial_w[l], spatial_b[l])
        y_b = stick_breaking_attention(q.reshape(bsz, seq, H_B, DH_B),
                                       k.reshape(bsz, seq, H_B, DH_B),
                                       v.reshape(bsz, seq, H_B, DH_B)).reshape(bsz, seq, W_B)
        y = jnp.concatenate([rmsnorm(y_a, gnorm_a[l]), rmsnorm(y_b, gnorm_b[l])], axis=-1)
        h = h + y @ w_mix_out[l]

        h = h + memory_cross_attention(rmsnorm(h, cross_norm[l]), rmsnorm(mem, mem_norm[l]),
                                       w_cq[l], w_ckv[l], w_co[l])

        h = h + 0.5 * swiglu(rmsnorm(h, ffn2_norm[l]), ffn2_w_in[l], ffn2_w_out[l])
    return rmsnorm(h, final_norm)
```

```python
import functools

import jax
import jax.numpy as jnp
from jax import lax
from jax.experimental import pallas as pl
from jax.experimental.pallas import tpu as pltpu

F32 = jnp.float32
BF16 = jnp.bfloat16

EPS = 1e-6
CHUNK = 64
N_MEM = 256
G_A = 8
GA_DIM = 128
SGU_BLOCK = 128
H_B = 8
DH_B = 128
Q_BLOCK = 128
X_HEADS = 4

_MIB = 1024 * 1024
_NT_DIMS = (((1,), (1,)), ((), ()))


def _rms(x, g):
    return x * lax.rsqrt(jnp.mean(x * x, axis=-1, keepdims=True) + EPS) * g


def _dot(a, b):
    return jnp.dot(a, b, preferred_element_type=F32)


def _dot_nt(a, b):
    return lax.dot_general(a, b, _NT_DIMS, preferred_element_type=F32)


def _params(semantics, vmem_mib):
    return pltpu.CompilerParams(dimension_semantics=semantics,
                                vmem_limit_bytes=vmem_mib * _MIB)


def _ffn_kernel(x_ref, g_ref, wg_ref, wu_ref, wo_ref, *rest, final):
    if final:
        fg_ref, o_ref, xn_ref = rest
    else:
        o_ref, xn_ref = rest
    f = pl.program_id(1)

    @pl.when(f == 0)
    def _():
        x = x_ref[...]
        xn_ref[...] = _rms(x, g_ref[...]).astype(BF16)
        o_ref[...] = x

    xn = xn_ref[...]
    gate = _dot(xn, wg_ref[...])
    up = _dot(xn, wu_ref[...])
    a = (0.5 * gate) * (1.0 / (1.0 + jnp.exp(-gate))) * up
    o_ref[...] += _dot(a.astype(BF16), wo_ref[...])

    if final:
        @pl.when(f == pl.num_programs(1) - 1)
        def _():
            o_ref[...] = _rms(o_ref[...], fg_ref[...])


def _ffn(x, g, w_in, w_out, final_g=None, *, tm=512, tf=512, name):
    t, d = x.shape
    d_ff = w_out.shape[0]
    nf = d_ff // tf
    assert t % tm == 0 and d_ff % tf == 0
    final = final_g is not None
    in_specs = [
        pl.BlockSpec((tm, d), lambda i, f: (i, 0)),
        pl.BlockSpec((1, d), lambda i, f: (0, 0)),
        pl.BlockSpec((d, tf), lambda i, f: (0, f)),
        pl.BlockSpec((d, tf), lambda i, f: (0, f + nf)),
        pl.BlockSpec((tf, d), lambda i, f: (f, 0)),
    ]
    args = [x, g, w_in, w_in, w_out]
    if final:
        in_specs.append(pl.BlockSpec((1, d), lambda i, f: (0, 0)))
        args.append(final_g)
    return pl.pallas_call(
        functools.partial(_ffn_kernel, final=final),
        grid=(t // tm, nf),
        in_specs=in_specs,
        out_specs=pl.BlockSpec((tm, d), lambda i, f: (i, 0)),
        out_shape=jax.ShapeDtypeStruct((t, d), F32),
        scratch_shapes=[pltpu.VMEM((tm, d), BF16)],
        compiler_params=_params(("parallel", "arbitrary"), 48),
        name=name,
    )(*args)


def _gelu_tanh(z):
    c = 0.7978845608028654
    return 0.5 * z * (1.0 + jnp.tanh(c * (z + 0.044715 * (z * z * z))))


def _proj_kernel(x_ref, g_ref, w_ref, o_ref, xn_ref, *, gelu):
    @pl.when(pl.program_id(1) == 0)
    def _():
        xn_ref[...] = _rms(x_ref[...], g_ref[...]).astype(BF16)

    z = _dot(xn_ref[...], w_ref[...])
    if gelu:
        z = _gelu_tanh(z)
    o_ref[...] = z.astype(o_ref.dtype)


def _proj(x, g, w, *, col0, ncols, out_dtype, gelu, tm=512, tn=1024, name):
    t, d = x.shape
    assert t % tm == 0 and ncols % tn == 0 and col0 % tn == 0
    nb0 = col0 // tn
    return pl.pallas_call(
        functools.partial(_proj_kernel, gelu=gelu),
        grid=(t // tm, ncols // tn),
        in_specs=[
            pl.BlockSpec((tm, d), lambda i, n: (i, 0)),
            pl.BlockSpec((1, d), lambda i, n: (0, 0)),
            pl.BlockSpec((d, tn), lambda i, n: (0, n + nb0)),
        ],
        out_specs=pl.BlockSpec((tm, tn), lambda i, n: (i, n)),
        out_shape=jax.ShapeDtypeStruct((t, ncols), out_dtype),
        scratch_shapes=[pltpu.VMEM((tm, d), BF16)],
        compiler_params=_params(("parallel", "arbitrary"), 40),
        name=name,
    )(x, g, w)


def _sgu_kernel(za_ref, lng_ref, lnb_ref, ws_ref, bs_ref, gn_ref, o_ref, y_ref):
    ts, w_a = o_ref.shape
    row_chunk = lax.broadcasted_iota(jnp.int32, (SGU_BLOCK, SGU_BLOCK), 0) // CHUNK
    col_chunk = lax.broadcasted_iota(jnp.int32, (SGU_BLOCK, SGU_BLOCK), 1) // CHUNK
    chunk_causal = col_chunk <= row_chunk
    w_masked = [jnp.where(chunk_causal, ws_ref[g], 0.0).astype(BF16) for g in range(G_A)]

    for blk in range(ts // SGU_BLOCK):
        rows = slice(blk * SGU_BLOCK, (blk + 1) * SGU_BLOCK)
        ssq = jnp.zeros((SGU_BLOCK, 1), F32)
        for g in range(G_A):
            cu = slice(g * GA_DIM, (g + 1) * GA_DIM)
            cv = slice(w_a + g * GA_DIM, w_a + (g + 1) * GA_DIM)
            v = za_ref[rows, cv]
            mu = jnp.mean(v, axis=-1, keepdims=True)
            dv = v - mu
            var = jnp.mean(dv * dv, axis=-1, keepdims=True)
            vn = dv * lax.rsqrt(var + EPS) * lng_ref[:, cu] + lnb_ref[:, cu]
            mixed = _dot(w_masked[g], vn.astype(BF16)) + bs_ref[:, g:g + 1]
            y = za_ref[rows, cu] * mixed
            ssq = ssq + jnp.sum(y * y, axis=-1, keepdims=True)
            y_ref[:, cu] = y
        inv = lax.rsqrt(ssq * (1.0 / w_a) + EPS)
        o_ref[rows, :] = (y_ref[...] * inv * gn_ref[...]).astype(o_ref.dtype)


def _sgu(za, ln_g, ln_b, w_s, b_s_t, gn, *, ts=256, name):
    t, w2 = za.shape
    w_a = w2 // 2
    assert t % ts == 0 and ts % SGU_BLOCK == 0
    const = lambda *shape: pl.BlockSpec(shape, lambda i: (0,) * len(shape))
    return pl.pallas_call(
        _sgu_kernel,
        grid=(t // ts,),
        in_specs=[
            pl.BlockSpec((ts, w2), lambda i: (i, 0)),
            const(1, w_a), const(1, w_a),
            const(G_A, SGU_BLOCK, SGU_BLOCK),
            const(SGU_BLOCK, G_A),
            const(1, w_a),
        ],
        out_specs=pl.BlockSpec((ts, w_a), lambda i: (i, 0)),
        out_shape=jax.ShapeDtypeStruct((t, w_a), BF16),
        scratch_shapes=[pltpu.VMEM((SGU_BLOCK, w_a), F32)],
        compiler_params=_params(("parallel",), 32),
        name=name,
    )(za, ln_g, ln_b, w_s, b_s_t, gn)


def _sb_kernel(q_ref, k_ref, v_ref, gn_ref, o_ref, y_ref):
    i = pl.program_id(1)
    w_b = o_ref.shape[1]
    r = lax.broadcasted_iota(jnp.int32, (Q_BLOCK, Q_BLOCK), 0)
    c = lax.broadcasted_iota(jnp.int32, (Q_BLOCK, Q_BLOCK), 1)
    causal = c < r
    suffix_total = jnp.concatenate(
        [jnp.where(r > c, 1.0, 0.0), jnp.ones((Q_BLOCK, Q_BLOCK), F32)], axis=1).astype(BF16)
    scale = DH_B ** -0.5

    ssq = jnp.zeros((Q_BLOCK, 1), F32)
    for h in range(H_B):
        cols = slice(h * DH_B, (h + 1) * DH_B)
        qh = q_ref[:, cols]

        def key_block(j, acc, carry, diagonal):
            rows = pl.ds(pl.multiple_of(j * Q_BLOCK, Q_BLOCK), Q_BLOCK)
            z = _dot_nt(qh, k_ref[rows, cols]) * scale
            lp = jnp.log1p(jnp.exp(-jnp.abs(z)))
            log_beta = jnp.minimum(z, 0.0) - lp
            log_1m = -jnp.maximum(z, 0.0) - lp
            if diagonal:
                log_1m = jnp.where(causal, log_1m, 0.0)
            hi = log_1m.astype(BF16)
            lo = (log_1m - hi.astype(F32)).astype(BF16)
            sums = _dot(hi, suffix_total) + _dot(lo, suffix_total)
            a = jnp.exp(log_beta + sums[:, :Q_BLOCK] + carry)
            if diagonal:
                a = jnp.where(causal, a, 0.0)
            acc = acc + _dot(a.astype(BF16), v_ref[rows, cols])
            return acc, carry + sums[:, Q_BLOCK:]

        zeros = jnp.zeros((Q_BLOCK, Q_BLOCK), F32)
        acc, carry = key_block(i, zeros, zeros, True)

        def body(t, state):
            return key_block(i - 1 - t, state[0], state[1], False)

        acc, carry = lax.fori_loop(0, i, body, (acc, carry))
        y_ref[:, cols] = acc
        ssq = ssq + jnp.sum(acc * acc, axis=-1, keepdims=True)

    inv = lax.rsqrt(ssq * (1.0 / w_b) + EPS)
    o_ref[...] = (y_ref[...] * inv * gn_ref[...]).astype(o_ref.dtype)


def _sb(qkv, gn, *, bsz, seq, name):
    w_b = qkv.shape[1] // 3
    nq = seq // Q_BLOCK
    return pl.pallas_call(
        _sb_kernel,
        grid=(bsz, nq),
        in_specs=[
            pl.BlockSpec((Q_BLOCK, w_b), lambda b, i: (b * nq + i, 0)),
            pl.BlockSpec((seq, w_b), lambda b, i: (b, 1)),
            pl.BlockSpec((seq, w_b), lambda b, i: (b, 2)),
            pl.BlockSpec((1, w_b), lambda b, i: (0, 0)),
        ],
        out_specs=pl.BlockSpec((Q_BLOCK, w_b), lambda b, i: (b * nq + i, 0)),
        out_shape=jax.ShapeDtypeStruct((bsz * seq, w_b), BF16),
        scratch_shapes=[pltpu.VMEM((Q_BLOCK, w_b), F32)],
        compiler_params=_params(("parallel", "arbitrary"), 40),
        name=name,
    )(qkv, qkv, qkv, gn)


def _mixcross_kernel(h_ref, ya_ref, yb_ref, wmo_ref, cn_ref, wcq_ref, kv_ref, wco_ref, o_ref,
                     att_ref):
    d = h_ref.shape[1]
    w_a = ya_ref.shape[1]
    x_dh = d // X_HEADS
    h2 = h_ref[...] + _dot(ya_ref[...], wmo_ref[:w_a, :]) + _dot(yb_ref[...], wmo_ref[w_a:, :])
    xn = _rms(h2, cn_ref[...]).astype(BF16)
    q = (_dot(xn, wcq_ref[...]) * (x_dh ** -0.5)).astype(BF16)
    for hh in range(X_HEADS):
        cols = slice(hh * x_dh, (hh + 1) * x_dh)
        s = _dot_nt(q[:, cols], kv_ref[:, cols])
        p = jnp.exp(s - jnp.max(s, axis=-1, keepdims=True))
        p = p / jnp.sum(p, axis=-1, keepdims=True)
        oh = _dot(p.astype(BF16), kv_ref[:, d + hh * x_dh:d + (hh + 1) * x_dh])
        att_ref[:, cols] = oh.astype(BF16)
    o_ref[...] = h2 + _dot(att_ref[...], wco_ref[...])


def _mixcross(h, ya, yb, w_mo, cn, w_cq, kv, w_co, *, seq, tm=256, name):
    t, d = h.shape
    w_a = ya.shape[1]
    assert seq % tm == 0
    tiles_per_seq = seq // tm
    resident = lambda shape: pl.BlockSpec(shape, lambda i: (0, 0), pipeline_mode=pl.Buffered(1))
    return pl.pallas_call(
        _mixcross_kernel,
        grid=(t // tm,),
        in_specs=[
            pl.BlockSpec((tm, d), lambda i: (i, 0)),
            pl.BlockSpec((tm, w_a), lambda i: (i, 0)),
            pl.BlockSpec((tm, w_a), lambda i: (i, 0)),
            resident((d, d)),
            pl.BlockSpec((1, d), lambda i: (0, 0)),
            resident((d, d)),
            pl.BlockSpec((N_MEM, 2 * d), lambda i: (i // tiles_per_seq, 0)),
            resident((d, d)),
        ],
        out_specs=pl.BlockSpec((tm, d), lambda i: (i, 0)),
        out_shape=jax.ShapeDtypeStruct((t, d), F32),
        scratch_shapes=[pltpu.VMEM((tm, d), BF16)],
        compiler_params=_params(("parallel",), 56),
        name=name,
    )(h, ya, yb, w_mo, cn, w_cq, kv, w_co)


def kernel(x, mem, ffn1_norm, ffn1_w_in, ffn1_w_out, mix_norm, w_mix_in, ln_v_gain, ln_v_bias,
           spatial_w, spatial_b, gnorm_a, gnorm_b, w_mix_out, cross_norm, mem_norm, w_cq, w_ckv,
           w_co, ffn2_norm, ffn2_w_in, ffn2_w_out, final_norm):
    bsz, seq, d = x.shape
    depth = ffn1_norm.shape[0]
    assert depth >= 1
    w_a = gnorm_a.shape[1]
    w_b = gnorm_b.shape[1]
    h = x.reshape(bsz * seq, d)
    mem2 = mem.reshape(bsz * N_MEM, d)
    row = lambda v: v.reshape(1, -1)
    for l in range(depth):
        h = _ffn(h, row(ffn1_norm[l]), ffn1_w_in[l].astype(BF16), ffn1_w_out[l].astype(BF16),
                 name="ffn1")
        w_in = w_mix_in[l].astype(BF16)
        za = _proj(h, row(mix_norm[l]), w_in, col0=0, ncols=2 * w_a, out_dtype=F32, gelu=True,
                   name="mix_in_za")
        qkv = _proj(h, row(mix_norm[l]), w_in, col0=2 * w_a, ncols=3 * w_b, out_dtype=BF16,
                    gelu=False, name="mix_in_qkv")
        ya = _sgu(za, row(ln_v_gain[l]), row(ln_v_bias[l]), spatial_w[l], spatial_b[l].T,
                  row(gnorm_a[l]), name="sgu")
        yb = _sb(qkv, row(gnorm_b[l]), bsz=bsz, seq=seq, name="stickbreak")
        kv = _proj(mem2, row(mem_norm[l]), w_ckv[l].astype(BF16), col0=0, ncols=2 * d,
                   out_dtype=BF16, gelu=False, name="mem_kv")
        h = _mixcross(h, ya, yb, w_mix_out[l].astype(BF16), row(cross_norm[l]),
                      w_cq[l].astype(BF16), kv, w_co[l].astype(BF16), seq=seq, name="mixcross")
        last = l == depth - 1
        h = _ffn(h, row(ffn2_norm[l]), ffn2_w_in[l].astype(BF16), ffn2_w_out[l].astype(BF16),
                 row(final_norm) if last else None, name="ffn2")
    return h.reshape(bsz, seq, d)
```

```python
import functools

import jax
import jax.numpy as jnp
from jax import lax
from jax.experimental import pallas as pl
from jax.experimental.pallas import tpu as pltpu

F32 = jnp.float32
BF16 = jnp.bfloat16

EPS = 1e-6
CHUNK = 64
N_MEM = 256
G_A = 8
GA_DIM = 128
SGU_BLOCK = 128
H_B = 8
DH_B = 128
Q_BLOCK = 128
KEY_CHUNK = 256
X_HEADS = 4

_MIB = 1024 * 1024
_NT_DIMS = (((1,), (1,)), ((), ()))


def _rms(x, g):
    return x * lax.rsqrt(jnp.mean(x * x, axis=-1, keepdims=True) + EPS) * g


def _dot(a, b):
    return jnp.dot(a, b, preferred_element_type=F32)


def _dot_nt(a, b):
    return lax.dot_general(a, b, _NT_DIMS, preferred_element_type=F32)


def _params(semantics, vmem_mib):
    return pltpu.CompilerParams(dimension_semantics=semantics,
                                vmem_limit_bytes=vmem_mib * _MIB)


def _ffn_kernel(x_ref, g_ref, wg_ref, wu_ref, wo_ref, *rest, final):
    if final:
        fg_ref, o_ref, xn_ref = rest
    else:
        o_ref, xn_ref = rest
    f = pl.program_id(1)

    @pl.when(f == 0)
    def _():
        x = x_ref[...]
        xn_ref[...] = _rms(x, g_ref[...]).astype(BF16)
        o_ref[...] = x

    xn = xn_ref[...]
    gate = _dot(xn, wg_ref[...])
    up = _dot(xn, wu_ref[...])
    a = (0.5 * gate) * (1.0 / (1.0 + jnp.exp(-gate))) * up
    o_ref[...] += _dot(a.astype(BF16), wo_ref[...])

    if final:
        @pl.when(f == pl.num_programs(1) - 1)
        def _():
            o_ref[...] = _rms(o_ref[...], fg_ref[...])


def _ffn(x, g, w_in, w_out, final_g=None, *, tm=1024, tf=512, name):
    t, d = x.shape
    d_ff = w_out.shape[0]
    nf = d_ff // tf
    assert t % tm == 0 and d_ff % tf == 0
    final = final_g is not None
    in_specs = [
        pl.BlockSpec((tm, d), lambda i, f: (i, 0)),
        pl.BlockSpec((1, d), lambda i, f: (0, 0)),
        pl.BlockSpec((d, tf), lambda i, f: (0, f)),
        pl.BlockSpec((d, tf), lambda i, f: (0, f + nf)),
        pl.BlockSpec((tf, d), lambda i, f: (f, 0)),
    ]
    args = [x, g, w_in, w_in, w_out]
    if final:
        in_specs.append(pl.BlockSpec((1, d), lambda i, f: (0, 0)))
        args.append(final_g)
    return pl.pallas_call(
        functools.partial(_ffn_kernel, final=final),
        grid=(t // tm, nf),
        in_specs=in_specs,
        out_specs=pl.BlockSpec((tm, d), lambda i, f: (i, 0)),
        out_shape=jax.ShapeDtypeStruct((t, d), F32),
        scratch_shapes=[pltpu.VMEM((tm, d), BF16)],
        compiler_params=_params(("parallel", "arbitrary"), 60),
        name=name,
    )(*args)


def _gelu_tanh(z):
    c = 0.7978845608028654
    return 0.5 * z * (1.0 + jnp.tanh(c * (z + 0.044715 * (z * z * z))))


def _proj_kernel(x_ref, g_ref, w_ref, o_ref, xn_ref, *, gelu, first_tile_scale):
    n = pl.program_id(1)

    @pl.when(n == 0)
    def _():
        xn_ref[...] = _rms(x_ref[...], g_ref[...]).astype(BF16)

    z = _dot(xn_ref[...], w_ref[...])
    if gelu:
        z = _gelu_tanh(z)
    if first_tile_scale is not None:
        z = z * jnp.where(n == 0, first_tile_scale, 1.0)
    o_ref[...] = z.astype(o_ref.dtype)


def _proj(x, g, w, *, col0, ncols, out_dtype, gelu, first_tile_scale=None, tm=512, tn=1024, name):
    t, d = x.shape
    assert t % tm == 0 and ncols % tn == 0 and col0 % tn == 0
    nb0 = col0 // tn
    return pl.pallas_call(
        functools.partial(_proj_kernel, gelu=gelu, first_tile_scale=first_tile_scale),
        grid=(t // tm, ncols // tn),
        in_specs=[
            pl.BlockSpec((tm, d), lambda i, n: (i, 0)),
            pl.BlockSpec((1, d), lambda i, n: (0, 0)),
            pl.BlockSpec((d, tn), lambda i, n: (0, n + nb0)),
        ],
        out_specs=pl.BlockSpec((tm, tn), lambda i, n: (i, n)),
        out_shape=jax.ShapeDtypeStruct((t, ncols), out_dtype),
        scratch_shapes=[pltpu.VMEM((tm, d), BF16)],
        compiler_params=_params(("parallel", "arbitrary"), 40),
        name=name,
    )(x, g, w)


def _sgu_kernel(za_ref, lng_ref, lnb_ref, ws_ref, bs_ref, gn_ref, o_ref, y_ref):
    ts, w_a = o_ref.shape
    row_chunk = lax.broadcasted_iota(jnp.int32, (SGU_BLOCK, SGU_BLOCK), 0) // CHUNK
    col_chunk = lax.broadcasted_iota(jnp.int32, (SGU_BLOCK, SGU_BLOCK), 1) // CHUNK
    chunk_causal = col_chunk <= row_chunk
    w_masked = [jnp.where(chunk_causal, ws_ref[g], 0.0).astype(BF16) for g in range(G_A)]

    for blk in range(ts // SGU_BLOCK):
        rows = slice(blk * SGU_BLOCK, (blk + 1) * SGU_BLOCK)
        ssq = jnp.zeros((SGU_BLOCK, 1), F32)
        for g in range(G_A):
            cu = slice(g * GA_DIM, (g + 1) * GA_DIM)
            cv = slice(w_a + g * GA_DIM, w_a + (g + 1) * GA_DIM)
            v = za_ref[rows, cv]
            mu = jnp.mean(v, axis=-1, keepdims=True)
            dv = v - mu
            var = jnp.mean(dv * dv, axis=-1, keepdims=True)
            vn = dv * lax.rsqrt(var + EPS) * lng_ref[:, cu] + lnb_ref[:, cu]
            mixed = _dot(w_masked[g], vn.astype(BF16)) + bs_ref[:, g:g + 1]
            y = za_ref[rows, cu] * mixed
            ssq = ssq + jnp.sum(y * y, axis=-1, keepdims=True)
            y_ref[:, cu] = y
        inv = lax.rsqrt(ssq * (1.0 / w_a) + EPS)
        o_ref[rows, :] = (y_ref[...] * inv * gn_ref[...]).astype(o_ref.dtype)


def _sgu(za, ln_g, ln_b, w_s, b_s_t, gn, *, ts=256, name):
    t, w2 = za.shape
    w_a = w2 // 2
    assert t % ts == 0 and ts % SGU_BLOCK == 0
    const = lambda *shape: pl.BlockSpec(shape, lambda i: (0,) * len(shape))
    return pl.pallas_call(
        _sgu_kernel,
        grid=(t // ts,),
        in_specs=[
            pl.BlockSpec((ts, w2), lambda i: (i, 0)),
            const(1, w_a), const(1, w_a),
            const(G_A, SGU_BLOCK, SGU_BLOCK),
            const(SGU_BLOCK, G_A),
            const(1, w_a),
        ],
        out_specs=pl.BlockSpec((ts, w_a), lambda i: (i, 0)),
        out_shape=jax.ShapeDtypeStruct((t, w_a), BF16),
        scratch_shapes=[pltpu.VMEM((SGU_BLOCK, w_a), F32)],
        compiler_params=_params(("parallel",), 32),
        name=name,
    )(za, ln_g, ln_b, w_s, b_s_t, gn)


def _sb_kernel(q_ref, k_ref, v_ref, gn_ref, o_ref, acc_ref, carry_ref):
    i = pl.program_id(1)
    w_b = o_ref.shape[1]
    stacked = H_B * Q_BLOCK
    head_rows = [slice(h * Q_BLOCK, (h + 1) * Q_BLOCK) for h in range(H_B)]
    head_cols = [slice(h * DH_B, (h + 1) * DH_B) for h in range(H_B)]
    r = lax.broadcasted_iota(jnp.int32, (KEY_CHUNK, KEY_CHUNK), 0)
    c = lax.broadcasted_iota(jnp.int32, (KEY_CHUNK, KEY_CHUNK), 1)
    suffix_total = jnp.concatenate(
        [jnp.where(r > c, 1.0, 0.0), jnp.ones((KEY_CHUNK, Q_BLOCK), F32)], axis=1).astype(BF16)

    acc_ref[...] = jnp.zeros_like(acc_ref)
    carry_ref[...] = jnp.zeros_like(carry_ref)
    diag_chunk = (i * Q_BLOCK) // KEY_CHUNK

    def key_chunk(jc, diagonal):
        rows = pl.ds(pl.multiple_of(jc * KEY_CHUNK, KEY_CHUNK), KEY_CHUNK)
        z = jnp.concatenate(
            [_dot_nt(q_ref[:, head_cols[h]], k_ref[rows, head_cols[h]]) for h in range(H_B)], axis=0)
        lp = jnp.log(1.0 + jnp.exp(-jnp.abs(z)))
        log_beta = jnp.minimum(z, 0.0) - lp
        log_1m = log_beta - z
        if diagonal:
            q_pos = i * Q_BLOCK + (
                lax.broadcasted_iota(jnp.int32, (stacked, KEY_CHUNK), 0) & (Q_BLOCK - 1))
            k_pos = jc * KEY_CHUNK + lax.broadcasted_iota(jnp.int32, (stacked, KEY_CHUNK), 1)
            causal = k_pos < q_pos
            log_1m = jnp.where(causal, log_1m, 0.0)
        hi = log_1m.astype(BF16)
        lo = (log_1m - hi.astype(F32)).astype(BF16)
        sums = _dot(hi, suffix_total) + _dot(lo, suffix_total)
        carry = carry_ref[...]
        a = jnp.exp(log_beta + sums[:, :KEY_CHUNK] + jnp.concatenate([carry, carry], axis=1))
        if diagonal:
            a = jnp.where(causal, a, 0.0)
        a = a.astype(BF16)
        for h in range(H_B):
            acc_ref[head_rows[h], :] += _dot(a[head_rows[h], :], v_ref[rows, head_cols[h]])
        carry_ref[...] = carry + sums[:, KEY_CHUNK:]

    key_chunk(diag_chunk, True)

    def body(t, _):
        key_chunk(diag_chunk - 1 - t, False)
        return 0

    lax.fori_loop(0, diag_chunk, body, 0)

    ssq = jnp.zeros((Q_BLOCK, 1), F32)
    for h in range(H_B):
        y = acc_ref[head_rows[h], :]
        ssq = ssq + jnp.sum(y * y, axis=-1, keepdims=True)
    inv = lax.rsqrt(ssq * (1.0 / w_b) + EPS)
    for h in range(H_B):
        o_ref[:, head_cols[h]] = (
            acc_ref[head_rows[h], :] * inv * gn_ref[:, head_cols[h]]).astype(o_ref.dtype)


def _sb(qkv, gn, *, bsz, seq, name):
    w_b = qkv.shape[1] // 3
    nq = seq // Q_BLOCK
    assert w_b == H_B * DH_B and seq % KEY_CHUNK == 0
    return pl.pallas_call(
        _sb_kernel,
        grid=(bsz, nq),
        in_specs=[
            pl.BlockSpec((Q_BLOCK, w_b), lambda b, i: (b * nq + i, 0)),
            pl.BlockSpec((seq, w_b), lambda b, i: (b, 1)),
            pl.BlockSpec((seq, w_b), lambda b, i: (b, 2)),
            pl.BlockSpec((1, w_b), lambda b, i: (0, 0)),
        ],
        out_specs=pl.BlockSpec((Q_BLOCK, w_b), lambda b, i: (b * nq + i, 0)),
        out_shape=jax.ShapeDtypeStruct((bsz * seq, w_b), BF16),
        scratch_shapes=[pltpu.VMEM((H_B * Q_BLOCK, DH_B), F32),
                        pltpu.VMEM((H_B * Q_BLOCK, Q_BLOCK), F32)],
        compiler_params=_params(("parallel", "arbitrary"), 40),
        name=name,
    )(qkv, qkv, qkv, gn)


def _mixcross_kernel(h_ref, ya_ref, yb_ref, wmo_ref, cn_ref, wcq_ref, kv_ref, wco_ref, o_ref,
                     att_ref):
    d = h_ref.shape[1]
    w_a = ya_ref.shape[1]
    x_dh = d // X_HEADS
    h2 = h_ref[...] + _dot(ya_ref[...], wmo_ref[:w_a, :]) + _dot(yb_ref[...], wmo_ref[w_a:, :])
    xn = _rms(h2, cn_ref[...]).astype(BF16)
    q = (_dot(xn, wcq_ref[...]) * (x_dh ** -0.5)).astype(BF16)
    for hh in range(X_HEADS):
        cols = slice(hh * x_dh, (hh + 1) * x_dh)
        s = _dot_nt(q[:, cols], kv_ref[:, cols])
        p = jnp.exp(s - jnp.max(s, axis=-1, keepdims=True))
        p = p / jnp.sum(p, axis=-1, keepdims=True)
        oh = _dot(p.astype(BF16), kv_ref[:, d + hh * x_dh:d + (hh + 1) * x_dh])
        att_ref[:, cols] = oh.astype(BF16)
    o_ref[...] = h2 + _dot(att_ref[...], wco_ref[...])


def _mixcross(h, ya, yb, w_mo, cn, w_cq, kv, w_co, *, seq, tm=256, name):
    t, d = h.shape
    w_a = ya.shape[1]
    assert seq % tm == 0
    tiles_per_seq = seq // tm
    resident = lambda shape: pl.BlockSpec(shape, lambda i: (0, 0), pipeline_mode=pl.Buffered(1))
    return pl.pallas_call(
        _mixcross_kernel,
        grid=(t // tm,),
        in_specs=[
            pl.BlockSpec((tm, d), lambda i: (i, 0)),
            pl.BlockSpec((tm, w_a), lambda i: (i, 0)),
            pl.BlockSpec((tm, w_a), lambda i: (i, 0)),
            resident((d, d)),
            pl.BlockSpec((1, d), lambda i: (0, 0)),
            resident((d, d)),
            pl.BlockSpec((N_MEM, 2 * d), lambda i: (i // tiles_per_seq, 0)),
            resident((d, d)),
        ],
        out_specs=pl.BlockSpec((tm, d), lambda i: (i, 0)),
        out_shape=jax.ShapeDtypeStruct((t, d), F32),
        scratch_shapes=[pltpu.VMEM((tm, d), BF16)],
        compiler_params=_params(("parallel",), 56),
        name=name,
    )(h, ya, yb, w_mo, cn, w_cq, kv, w_co)


def kernel(x, mem, ffn1_norm, ffn1_w_in, ffn1_w_out, mix_norm, w_mix_in, ln_v_gain, ln_v_bias,
           spatial_w, spatial_b, gnorm_a, gnorm_b, w_mix_out, cross_norm, mem_norm, w_cq, w_ckv,
           w_co, ffn2_norm, ffn2_w_in, ffn2_w_out, final_norm):
    bsz, seq, d = x.shape
    depth = ffn1_norm.shape[0]
    assert depth >= 1
    w_a = gnorm_a.shape[1]
    w_b = gnorm_b.shape[1]
    h = x.reshape(bsz * seq, d)
    mem2 = mem.reshape(bsz * N_MEM, d)
    row = lambda v: v.reshape(1, -1)
    for l in range(depth):
        h = _ffn(h, row(ffn1_norm[l]), ffn1_w_in[l].astype(BF16), ffn1_w_out[l].astype(BF16),
                 name="ffn1")
        w_in = w_mix_in[l].astype(BF16)
        za = _proj(h, row(mix_norm[l]), w_in, col0=0, ncols=2 * w_a, out_dtype=F32, gelu=True,
                   name="mix_in_za")
        assert w_b == 1024
        qkv = _proj(h, row(mix_norm[l]), w_in, col0=2 * w_a, ncols=3 * w_b, out_dtype=BF16,
                    gelu=False, first_tile_scale=DH_B ** -0.5, tn=w_b, name="mix_in_qkv")
        ya = _sgu(za, row(ln_v_gain[l]), row(ln_v_bias[l]), spatial_w[l], spatial_b[l].T,
                  row(gnorm_a[l]), name="sgu")
        yb = _sb(qkv, row(gnorm_b[l]), bsz=bsz, seq=seq, name="stickbreak")
        kv = _proj(mem2, row(mem_norm[l]), w_ckv[l].astype(BF16), col0=0, ncols=2 * d,
                   out_dtype=BF16, gelu=False, name="mem_kv")
        h = _mixcross(h, ya, yb, w_mix_out[l].astype(BF16), row(cross_norm[l]),
                      w_cq[l].astype(BF16), kv, w_co[l].astype(BF16), seq=seq, name="mixcross")
        last = l == depth - 1
        h = _ffn(h, row(ffn2_norm[l]), ffn2_w_in[l].astype(BF16), ffn2_w_out[l].astype(BF16),
                 row(final_norm) if last else None, name="ffn2")
    return h.reshape(bsz, seq, d)
```

```python
import functools

import jax
import jax.numpy as jnp
from jax import lax
from jax.experimental import pallas as pl
from jax.experimental.pallas import tpu as pltpu

F32 = jnp.float32
BF16 = jnp.bfloat16

EPS = 1e-6
CHUNK = 64
N_MEM = 256
G_A = 8
GA_DIM = 128
SGU_BLOCK = 128
H_B = 8
DH_B = 128
Q_BLOCK = 128
KEY_CHUNK = 256
EXP_UNDERFLOW = -104.0
X_HEADS = 4

_MIB = 1024 * 1024
_NT_DIMS = (((1,), (1,)), ((), ()))


def _rms(x, g):
    return x * lax.rsqrt(jnp.mean(x * x, axis=-1, keepdims=True) + EPS) * g


def _dot(a, b):
    return jnp.dot(a, b, preferred_element_type=F32)


def _dot_nt(a, b):
    return lax.dot_general(a, b, _NT_DIMS, preferred_element_type=F32)


def _params(semantics, vmem_mib):
    return pltpu.CompilerParams(dimension_semantics=semantics,
                                vmem_limit_bytes=vmem_mib * _MIB)


def _ffn_kernel(x_ref, g_ref, wg_ref, wu_ref, wo_ref, *rest, final):
    if final:
        fg_ref, o_ref, xn_ref = rest
    else:
        o_ref, xn_ref = rest
    f = pl.program_id(1)

    @pl.when(f == 0)
    def _():
        x = x_ref[...]
        xn_ref[...] = _rms(x, g_ref[...]).astype(BF16)
        o_ref[...] = x

    xn = xn_ref[...]
    gate = _dot(xn, wg_ref[...])
    up = _dot(xn, wu_ref[...])
    a = (0.5 * gate) * (1.0 / (1.0 + jnp.exp(-gate))) * up
    o_ref[...] += _dot(a.astype(BF16), wo_ref[...])

    if final:
        @pl.when(f == pl.num_programs(1) - 1)
        def _():
            o_ref[...] = _rms(o_ref[...], fg_ref[...])


def _ffn(x, g, w_in, w_out, final_g=None, *, tm=1024, tf=512, name):
    t, d = x.shape
    d_ff = w_out.shape[0]
    nf = d_ff // tf
    assert t % tm == 0 and d_ff % tf == 0
    final = final_g is not None
    in_specs = [
        pl.BlockSpec((tm, d), lambda i, f: (i, 0)),
        pl.BlockSpec((1, d), lambda i, f: (0, 0)),
        pl.BlockSpec((d, tf), lambda i, f: (0, f)),
        pl.BlockSpec((d, tf), lambda i, f: (0, f + nf)),
        pl.BlockSpec((tf, d), lambda i, f: (f, 0)),
    ]
    args = [x, g, w_in, w_in, w_out]
    if final:
        in_specs.append(pl.BlockSpec((1, d), lambda i, f: (0, 0)))
        args.append(final_g)
    return pl.pallas_call(
        functools.partial(_ffn_kernel, final=final),
        grid=(t // tm, nf),
        in_specs=in_specs,
        out_specs=pl.BlockSpec((tm, d), lambda i, f: (i, 0)),
        out_shape=jax.ShapeDtypeStruct((t, d), F32),
        scratch_shapes=[pltpu.VMEM((tm, d), BF16)],
        compiler_params=_params(("parallel", "arbitrary"), 60),
        name=name,
    )(*args)


def _proj_kernel(x_ref, g_ref, w_ref, o_ref, xn_ref):
    @pl.when(pl.program_id(1) == 0)
    def _():
        xn_ref[...] = _rms(x_ref[...], g_ref[...]).astype(BF16)

    o_ref[...] = _dot(xn_ref[...], w_ref[...]).astype(o_ref.dtype)


def _proj(x, g, w, *, out_dtype, tm=512, tn=1024, name):
    t, d = x.shape
    ncols = w.shape[1]
    assert t % tm == 0 and ncols % tn == 0
    return pl.pallas_call(
        _proj_kernel,
        grid=(t // tm, ncols // tn),
        in_specs=[
            pl.BlockSpec((tm, d), lambda i, n: (i, 0)),
            pl.BlockSpec((1, d), lambda i, n: (0, 0)),
            pl.BlockSpec((d, tn), lambda i, n: (0, n)),
        ],
        out_specs=pl.BlockSpec((tm, tn), lambda i, n: (i, n)),
        out_shape=jax.ShapeDtypeStruct((t, ncols), out_dtype),
        scratch_shapes=[pltpu.VMEM((tm, d), BF16)],
        compiler_params=_params(("parallel", "arbitrary"), 40),
        name=name,
    )(x, g, w)


def _gelu_tanh(z):
    c = 0.7978845608028654
    return 0.5 * z * (1.0 + jnp.tanh(c * (z + 0.044715 * (z * z * z))))


def _mixin_kernel(x_ref, g_ref, w_ref, za_ref, qkv_ref):
    n_za = za_ref.shape[1]
    w_b = qkv_ref.shape[1] // 3
    xn = _rms(x_ref[...], g_ref[...]).astype(BF16)
    za_ref[...] = _gelu_tanh(_dot(xn, w_ref[:, :n_za]))
    q = _dot(xn, w_ref[:, n_za:n_za + w_b]) * (DH_B ** -0.5)
    qkv_ref[:, :w_b] = q.astype(BF16)
    qkv_ref[:, w_b:] = _dot(xn, w_ref[:, n_za + w_b:]).astype(BF16)


def _mixin(x, g, w, *, n_za, tm=512, name):
    t, d = x.shape
    n_qkv = w.shape[1] - n_za
    assert t % tm == 0
    return pl.pallas_call(
        _mixin_kernel,
        grid=(t // tm,),
        in_specs=[
            pl.BlockSpec((tm, d), lambda i: (i, 0)),
            pl.BlockSpec((1, d), lambda i: (0, 0)),
            pl.BlockSpec(w.shape, lambda i: (0, 0), pipeline_mode=pl.Buffered(1)),
        ],
        out_specs=[pl.BlockSpec((tm, n_za), lambda i: (i, 0)),
                   pl.BlockSpec((tm, n_qkv), lambda i: (i, 0))],
        out_shape=[jax.ShapeDtypeStruct((t, n_za), F32),
                   jax.ShapeDtypeStruct((t, n_qkv), BF16)],
        compiler_params=_params(("parallel",), 56),
        name=name,
    )(x, g, w)


def _sgu_kernel(za_ref, lng_ref, lnb_ref, ws_ref, bs_ref, gn_ref, o_ref):
    ts, w_a = o_ref.shape
    row_chunk = lax.broadcasted_iota(jnp.int32, (SGU_BLOCK, SGU_BLOCK), 0) // CHUNK
    col_chunk = lax.broadcasted_iota(jnp.int32, (SGU_BLOCK, SGU_BLOCK), 1) // CHUNK
    chunk_causal = col_chunk <= row_chunk
    w_masked = [jnp.where(chunk_causal, ws_ref[g], 0.0).astype(BF16) for g in range(G_A)]
    group_cols = [slice(g * GA_DIM, (g + 1) * GA_DIM) for g in range(G_A)]
    group_rows = [slice(g * SGU_BLOCK, (g + 1) * SGU_BLOCK) for g in range(G_A)]

    for blk in range(ts // SGU_BLOCK):
        rows = slice(blk * SGU_BLOCK, (blk + 1) * SGU_BLOCK)
        v = jnp.concatenate(
            [za_ref[rows, w_a + g * GA_DIM:w_a + (g + 1) * GA_DIM] for g in range(G_A)], axis=0)
        dv = v - jnp.mean(v, axis=-1, keepdims=True)
        dn = dv * lax.rsqrt(jnp.mean(dv * dv, axis=-1, keepdims=True) + EPS)
        ys = []
        y_sq = jnp.zeros((SGU_BLOCK, GA_DIM), F32)
        for g in range(G_A):
            vn = dn[group_rows[g], :] * lng_ref[:, group_cols[g]] + lnb_ref[:, group_cols[g]]
            mixed = _dot(w_masked[g], vn.astype(BF16)) + bs_ref[group_rows[g], :]
            y = za_ref[rows, group_cols[g]] * mixed
            y_sq = y_sq + y * y
            ys.append(y)
        inv = lax.rsqrt(jnp.sum(y_sq, axis=-1, keepdims=True) * (1.0 / w_a) + EPS)
        for g in range(G_A):
            o_ref[rows, group_cols[g]] = (ys[g] * inv * gn_ref[:, group_cols[g]]).astype(o_ref.dtype)


def _sgu(za, ln_g, ln_b, w_s, b_s_rows, gn, *, ts=512, name):
    t, w2 = za.shape
    w_a = w2 // 2
    assert t % ts == 0 and ts % SGU_BLOCK == 0 and w_a == G_A * GA_DIM
    const = lambda *shape: pl.BlockSpec(shape, lambda i: (0,) * len(shape))
    return pl.pallas_call(
        _sgu_kernel,
        grid=(t // ts,),
        in_specs=[
            pl.BlockSpec((ts, w2), lambda i: (i, 0)),
            const(1, w_a), const(1, w_a),
            const(G_A, SGU_BLOCK, SGU_BLOCK),
            const(G_A * SGU_BLOCK, GA_DIM),
            const(1, w_a),
        ],
        out_specs=pl.BlockSpec((ts, w_a), lambda i: (i, 0)),
        out_shape=jax.ShapeDtypeStruct((t, w_a), BF16),
        compiler_params=_params(("parallel",), 32),
        name=name,
    )(za, ln_g, ln_b, w_s, b_s_rows, gn)


def _sb_kernel(q_ref, k_ref, v_ref, gn_ref, o_ref, acc_ref, carry_ref):
    i = pl.program_id(1)
    w_b = o_ref.shape[1]
    stacked = H_B * Q_BLOCK
    head_rows = [slice(h * Q_BLOCK, (h + 1) * Q_BLOCK) for h in range(H_B)]
    head_cols = [slice(h * DH_B, (h + 1) * DH_B) for h in range(H_B)]
    r = lax.broadcasted_iota(jnp.int32, (KEY_CHUNK, KEY_CHUNK), 0)
    c = lax.broadcasted_iota(jnp.int32, (KEY_CHUNK, KEY_CHUNK), 1)
    suffix_total = jnp.concatenate(
        [jnp.where(r > c, 1.0, 0.0), jnp.ones((KEY_CHUNK, Q_BLOCK), F32)], axis=1).astype(BF16)

    acc_ref[...] = jnp.zeros_like(acc_ref)
    carry_ref[...] = jnp.zeros_like(carry_ref)
    diag_chunk = (i * Q_BLOCK) // KEY_CHUNK

    def key_chunk(jc, diagonal):
        rows = pl.ds(pl.multiple_of(jc * KEY_CHUNK, KEY_CHUNK), KEY_CHUNK)
        z = jnp.concatenate(
            [_dot_nt(q_ref[:, head_cols[h]], k_ref[rows, head_cols[h]]) for h in range(H_B)], axis=0)
        lp = jnp.log(1.0 + jnp.exp(-jnp.abs(z)))
        log_beta = jnp.minimum(z, 0.0) - lp
        log_1m = log_beta - z
        if diagonal:
            q_pos = i * Q_BLOCK + (
                lax.broadcasted_iota(jnp.int32, (stacked, KEY_CHUNK), 0) & (Q_BLOCK - 1))
            k_pos = jc * KEY_CHUNK + lax.broadcasted_iota(jnp.int32, (stacked, KEY_CHUNK), 1)
            causal = k_pos < q_pos
            log_1m = jnp.where(causal, log_1m, 0.0)
        hi = log_1m.astype(BF16)
        lo = (log_1m - hi.astype(F32)).astype(BF16)
        sums = _dot(hi, suffix_total) + _dot(lo, suffix_total)
        carry = carry_ref[...]
        a = jnp.exp(log_beta + sums[:, :KEY_CHUNK] + jnp.concatenate([carry, carry], axis=1))
        if diagonal:
            a = jnp.where(causal, a, 0.0)
        a = a.astype(BF16)
        for h in range(H_B):
            acc_ref[head_rows[h], :] += _dot(a[head_rows[h], :], v_ref[rows, head_cols[h]])
        carry = carry + sums[:, KEY_CHUNK:]
        carry_ref[...] = carry
        return (jnp.max(carry) > EXP_UNDERFLOW).astype(jnp.int32)

    def cond(state):
        jc, live = state
        return jnp.logical_and(jc >= 0, live > 0)

    def body(state):
        jc, _ = state
        return jc - 1, key_chunk(jc, False)

    lax.while_loop(cond, body, (diag_chunk - 1, key_chunk(diag_chunk, True)))

    ssq = jnp.zeros((Q_BLOCK, 1), F32)
    for h in range(H_B):
        y = acc_ref[head_rows[h], :]
        ssq = ssq + jnp.sum(y * y, axis=-1, keepdims=True)
    inv = lax.rsqrt(ssq * (1.0 / w_b) + EPS)
    for h in range(H_B):
        o_ref[:, head_cols[h]] = (
            acc_ref[head_rows[h], :] * inv * gn_ref[:, head_cols[h]]).astype(o_ref.dtype)


def _sb(qkv, gn, *, bsz, seq, name):
    w_b = qkv.shape[1] // 3
    nq = seq // Q_BLOCK
    assert w_b == H_B * DH_B and seq % KEY_CHUNK == 0
    return pl.pallas_call(
        _sb_kernel,
        grid=(bsz, nq),
        in_specs=[
            pl.BlockSpec((Q_BLOCK, w_b), lambda b, i: (b * nq + i, 0)),
            pl.BlockSpec((seq, w_b), lambda b, i: (b, 1)),
            pl.BlockSpec((seq, w_b), lambda b, i: (b, 2)),
            pl.BlockSpec((1, w_b), lambda b, i: (0, 0)),
        ],
        out_specs=pl.BlockSpec((Q_BLOCK, w_b), lambda b, i: (b * nq + i, 0)),
        out_shape=jax.ShapeDtypeStruct((bsz * seq, w_b), BF16),
        scratch_shapes=[pltpu.VMEM((H_B * Q_BLOCK, DH_B), F32),
                        pltpu.VMEM((H_B * Q_BLOCK, Q_BLOCK), F32)],
        compiler_params=_params(("parallel", "arbitrary"), 40),
        name=name,
    )(qkv, qkv, qkv, gn)


def _mixcross_kernel(h_ref, ya_ref, yb_ref, wmo_ref, cn_ref, wcq_ref, kv_ref, wco_ref, o_ref,
                     att_ref):
    d = h_ref.shape[1]
    w_a = ya_ref.shape[1]
    x_dh = d // X_HEADS
    h2 = h_ref[...] + _dot(ya_ref[...], wmo_ref[:w_a, :]) + _dot(yb_ref[...], wmo_ref[w_a:, :])
    xn = _rms(h2, cn_ref[...]).astype(BF16)
    q = (_dot(xn, wcq_ref[...]) * (x_dh ** -0.5)).astype(BF16)
    for hh in range(X_HEADS):
        cols = slice(hh * x_dh, (hh + 1) * x_dh)
        s = _dot_nt(q[:, cols], kv_ref[:, cols])
        p = jnp.exp(s - jnp.max(s, axis=-1, keepdims=True))
        p = p / jnp.sum(p, axis=-1, keepdims=True)
        oh = _dot(p.astype(BF16), kv_ref[:, d + hh * x_dh:d + (hh + 1) * x_dh])
        att_ref[:, cols] = oh.astype(BF16)
    o_ref[...] = h2 + _dot(att_ref[...], wco_ref[...])


def _mixcross(h, ya, yb, w_mo, cn, w_cq, kv, w_co, *, seq, tm=256, name):
    t, d = h.shape
    w_a = ya.shape[1]
    assert seq % tm == 0
    tiles_per_seq = seq // tm
    resident = lambda shape: pl.BlockSpec(shape, lambda i: (0, 0), pipeline_mode=pl.Buffered(1))
    return pl.pallas_call(
        _mixcross_kernel,
        grid=(t // tm,),
        in_specs=[
            pl.BlockSpec((tm, d), lambda i: (i, 0)),
            pl.BlockSpec((tm, w_a), lambda i: (i, 0)),
            pl.BlockSpec((tm, w_a), lambda i: (i, 0)),
            resident((d, d)),
            pl.BlockSpec((1, d), lambda i: (0, 0)),
            resident((d, d)),
            pl.BlockSpec((N_MEM, 2 * d), lambda i: (i // tiles_per_seq, 0)),
            resident((d, d)),
        ],
        out_specs=pl.BlockSpec((tm, d), lambda i: (i, 0)),
        out_shape=jax.ShapeDtypeStruct((t, d), F32),
        scratch_shapes=[pltpu.VMEM((tm, d), BF16)],
        compiler_params=_params(("parallel",), 56),
        name=name,
    )(h, ya, yb, w_mo, cn, w_cq, kv, w_co)


def kernel(x, mem, ffn1_norm, ffn1_w_in, ffn1_w_out, mix_norm, w_mix_in, ln_v_gain, ln_v_bias,
           spatial_w, spatial_b, gnorm_a, gnorm_b, w_mix_out, cross_norm, mem_norm, w_cq, w_ckv,
           w_co, ffn2_norm, ffn2_w_in, ffn2_w_out, final_norm):
    bsz, seq, d = x.shape
    depth = ffn1_norm.shape[0]
    assert depth >= 1
    w_a = gnorm_a.shape[1]
    w_b = gnorm_b.shape[1]
    h = x.reshape(bsz * seq, d)
    mem2 = mem.reshape(bsz * N_MEM, d)
    row = lambda v: v.reshape(1, -1)
    for l in range(depth):
        h = _ffn(h, row(ffn1_norm[l]), ffn1_w_in[l].astype(BF16), ffn1_w_out[l].astype(BF16),
                 name="ffn1")
        za, qkv = _mixin(h, row(mix_norm[l]), w_mix_in[l].astype(BF16), n_za=2 * w_a, name="mix_in")
        b_s_rows = jnp.broadcast_to(spatial_b[l].reshape(G_A * SGU_BLOCK, 1), (G_A * SGU_BLOCK, GA_DIM))
        ya = _sgu(za, row(ln_v_gain[l]), row(ln_v_bias[l]), spatial_w[l], b_s_rows,
                  row(gnorm_a[l]), name="sgu")
        yb = _sb(qkv, row(gnorm_b[l]), bsz=bsz, seq=seq, name="stickbreak")
        kv = _proj(mem2, row(mem_norm[l]), w_ckv[l].astype(BF16), out_dtype=BF16, name="mem_kv")
        h = _mixcross(h, ya, yb, w_mix_out[l].astype(BF16), row(cross_norm[l]),
                      w_cq[l].astype(BF16), kv, w_co[l].astype(BF16), seq=seq, name="mixcross")
        last = l == depth - 1
        h = _ffn(h, row(ffn2_norm[l]), ffn2_w_in[l].astype(BF16), ffn2_w_out[l].astype(BF16),
                 row(final_norm) if last else None, name="ffn2")
    return h.reshape(bsz, seq, d)
```

```python
import functools

import jax
import jax.numpy as jnp
from jax import lax
from jax.experimental import pallas as pl
from jax.experimental.pallas import tpu as pltpu

F32 = jnp.float32
BF16 = jnp.bfloat16

EPS = 1e-6
CHUNK = 64
N_MEM = 256
G_A = 8
GA_DIM = 128
SGU_BLOCK = 128
H_B = 8
DH_B = 128
SB_TILE = 256
LOG2E = 1.4426950408889634
EXP2_UNDERFLOW = -150.0
X_HEADS = 4

_MIB = 1024 * 1024
_NT_DIMS = (((1,), (1,)), ((), ()))


def _rms(x, g):
    return x * lax.rsqrt(jnp.mean(x * x, axis=-1, keepdims=True) + EPS) * g


def _dot(a, b):
    return jnp.dot(a, b, preferred_element_type=F32)


def _dot_nt(a, b):
    return lax.dot_general(a, b, _NT_DIMS, preferred_element_type=F32)


def _params(semantics, vmem_mib):
    return pltpu.CompilerParams(dimension_semantics=semantics,
                                vmem_limit_bytes=vmem_mib * _MIB)


def _ffn_kernel(x_ref, g_ref, wg_ref, wu_ref, wo_ref, *rest, final):
    if final:
        fg_ref, o_ref, xn_ref = rest
    else:
        o_ref, xn_ref = rest
    f = pl.program_id(1)

    @pl.when(f == 0)
    def _():
        x = x_ref[...]
        xn_ref[...] = _rms(x, g_ref[...]).astype(BF16)
        o_ref[...] = x

    xn = xn_ref[...]
    gate = _dot(xn, wg_ref[...])
    up = _dot(xn, wu_ref[...])
    a = (0.5 * gate) * (1.0 / (1.0 + jnp.exp(-gate))) * up
    o_ref[...] += _dot(a.astype(BF16), wo_ref[...])

    if final:
        @pl.when(f == pl.num_programs(1) - 1)
        def _():
            o_ref[...] = _rms(o_ref[...], fg_ref[...])


def _ffn(x, g, w_in, w_out, final_g=None, *, tm=1024, tf=512, name):
    t, d = x.shape
    d_ff = w_out.shape[0]
    nf = d_ff // tf
    assert t % tm == 0 and d_ff % tf == 0
    final = final_g is not None
    in_specs = [
        pl.BlockSpec((tm, d), lambda i, f: (i, 0)),
        pl.BlockSpec((1, d), lambda i, f: (0, 0)),
        pl.BlockSpec((d, tf), lambda i, f: (0, f)),
        pl.BlockSpec((d, tf), lambda i, f: (0, f + nf)),
        pl.BlockSpec((tf, d), lambda i, f: (f, 0)),
    ]
    args = [x, g, w_in, w_in, w_out]
    if final:
        in_specs.append(pl.BlockSpec((1, d), lambda i, f: (0, 0)))
        args.append(final_g)
    return pl.pallas_call(
        functools.partial(_ffn_kernel, final=final),
        grid=(t // tm, nf),
        in_specs=in_specs,
        out_specs=pl.BlockSpec((tm, d), lambda i, f: (i, 0)),
        out_shape=jax.ShapeDtypeStruct((t, d), F32),
        scratch_shapes=[pltpu.VMEM((tm, d), BF16)],
        compiler_params=_params(("parallel", "arbitrary"), 60),
        name=name,
    )(*args)


def _proj_kernel(x_ref, g_ref, w_ref, o_ref, xn_ref):
    @pl.when(pl.program_id(1) == 0)
    def _():
        xn_ref[...] = _rms(x_ref[...], g_ref[...]).astype(BF16)

    o_ref[...] = _dot(xn_ref[...], w_ref[...]).astype(o_ref.dtype)


def _proj(x, g, w, *, out_dtype, tm=512, tn=1024, name):
    t, d = x.shape
    ncols = w.shape[1]
    assert t % tm == 0 and ncols % tn == 0
    return pl.pallas_call(
        _proj_kernel,
        grid=(t // tm, ncols // tn),
        in_specs=[
            pl.BlockSpec((tm, d), lambda i, n: (i, 0)),
            pl.BlockSpec((1, d), lambda i, n: (0, 0)),
            pl.BlockSpec((d, tn), lambda i, n: (0, n)),
        ],
        out_specs=pl.BlockSpec((tm, tn), lambda i, n: (i, n)),
        out_shape=jax.ShapeDtypeStruct((t, ncols), out_dtype),
        scratch_shapes=[pltpu.VMEM((tm, d), BF16)],
        compiler_params=_params(("parallel", "arbitrary"), 40),
        name=name,
    )(x, g, w)


def _gelu_tanh(z):
    c = 0.7978845608028654
    return 0.5 * z * (1.0 + jnp.tanh(c * (z + 0.044715 * (z * z * z))))


def _mixin_kernel(x_ref, g_ref, w_ref, za_ref, qkv_ref):
    n_za = za_ref.shape[1]
    w_b = qkv_ref.shape[1] // 3
    xn = _rms(x_ref[...], g_ref[...]).astype(BF16)
    za_ref[...] = _gelu_tanh(_dot(xn, w_ref[:, :n_za]))
    q = _dot(xn, w_ref[:, n_za:n_za + w_b]) * (LOG2E * DH_B ** -0.5)
    qkv_ref[:, :w_b] = q.astype(BF16)
    qkv_ref[:, w_b:] = _dot(xn, w_ref[:, n_za + w_b:]).astype(BF16)


def _mixin(x, g, w, *, n_za, tm=512, name):
    t, d = x.shape
    n_qkv = w.shape[1] - n_za
    assert t % tm == 0
    return pl.pallas_call(
        _mixin_kernel,
        grid=(t // tm,),
        in_specs=[
            pl.BlockSpec((tm, d), lambda i: (i, 0)),
            pl.BlockSpec((1, d), lambda i: (0, 0)),
            pl.BlockSpec(w.shape, lambda i: (0, 0), pipeline_mode=pl.Buffered(1)),
        ],
        out_specs=[pl.BlockSpec((tm, n_za), lambda i: (i, 0)),
                   pl.BlockSpec((tm, n_qkv), lambda i: (i, 0))],
        out_shape=[jax.ShapeDtypeStruct((t, n_za), F32),
                   jax.ShapeDtypeStruct((t, n_qkv), BF16)],
        compiler_params=_params(("parallel",), 56),
        name=name,
    )(x, g, w)


def _sgu_kernel(za_ref, lng_ref, lnb_ref, ws_ref, bs_ref, gn_ref, o_ref):
    ts, w_a = o_ref.shape
    row_chunk = lax.broadcasted_iota(jnp.int32, (SGU_BLOCK, SGU_BLOCK), 0) // CHUNK
    col_chunk = lax.broadcasted_iota(jnp.int32, (SGU_BLOCK, SGU_BLOCK), 1) // CHUNK
    chunk_causal = col_chunk <= row_chunk
    w_masked = [jnp.where(chunk_causal, ws_ref[g], 0.0).astype(BF16) for g in range(G_A)]
    group_cols = [slice(g * GA_DIM, (g + 1) * GA_DIM) for g in range(G_A)]
    group_rows = [slice(g * SGU_BLOCK, (g + 1) * SGU_BLOCK) for g in range(G_A)]

    for blk in range(ts // SGU_BLOCK):
        rows = slice(blk * SGU_BLOCK, (blk + 1) * SGU_BLOCK)
        v = jnp.concatenate(
            [za_ref[rows, w_a + g * GA_DIM:w_a + (g + 1) * GA_DIM] for g in range(G_A)], axis=0)
        dv = v - jnp.mean(v, axis=-1, keepdims=True)
        dn = dv * lax.rsqrt(jnp.mean(dv * dv, axis=-1, keepdims=True) + EPS)
        ys = []
        y_sq = jnp.zeros((SGU_BLOCK, GA_DIM), F32)
        for g in range(G_A):
            vn = dn[group_rows[g], :] * lng_ref[:, group_cols[g]] + lnb_ref[:, group_cols[g]]
            mixed = _dot(w_masked[g], vn.astype(BF16)) + bs_ref[group_rows[g], :]
            y = za_ref[rows, group_cols[g]] * mixed
            y_sq = y_sq + y * y
            ys.append(y)
        inv = lax.rsqrt(jnp.sum(y_sq, axis=-1, keepdims=True) * (1.0 / w_a) + EPS)
        for g in range(G_A):
            o_ref[rows, group_cols[g]] = (ys[g] * inv * gn_ref[:, group_cols[g]]).astype(o_ref.dtype)


def _sgu(za, ln_g, ln_b, w_s, b_s_rows, gn, *, ts=512, name):
    t, w2 = za.shape
    w_a = w2 // 2
    assert t % ts == 0 and ts % SGU_BLOCK == 0 and w_a == G_A * GA_DIM
    const = lambda *shape: pl.BlockSpec(shape, lambda i: (0,) * len(shape))
    return pl.pallas_call(
        _sgu_kernel,
        grid=(t // ts,),
        in_specs=[
            pl.BlockSpec((ts, w2), lambda i: (i, 0)),
            const(1, w_a), const(1, w_a),
            const(G_A, SGU_BLOCK, SGU_BLOCK),
            const(G_A * SGU_BLOCK, GA_DIM),
            const(1, w_a),
        ],
        out_specs=pl.BlockSpec((ts, w_a), lambda i: (i, 0)),
        out_shape=jax.ShapeDtypeStruct((t, w_a), BF16),
        compiler_params=_params(("parallel",), 32),
        name=name,
    )(za, ln_g, ln_b, w_s, b_s_rows, gn)


def _sb_kernel(q_ref, k_ref, v_ref, gn_ref, o_ref, acc_ref, carry_ref):
    i = pl.program_id(1)
    w_b = o_ref.shape[1]
    stacked = H_B * SB_TILE
    head_rows = [slice(h * SB_TILE, (h + 1) * SB_TILE) for h in range(H_B)]
    head_cols = [slice(h * DH_B, (h + 1) * DH_B) for h in range(H_B)]
    r = lax.broadcasted_iota(jnp.int32, (SB_TILE, SB_TILE), 0)
    c = lax.broadcasted_iota(jnp.int32, (SB_TILE, SB_TILE), 1)
    suffix = jnp.where(r > c, 1.0, 0.0).astype(BF16)
    suffix2 = jnp.concatenate([suffix, suffix], axis=0)

    acc_ref[...] = jnp.zeros_like(acc_ref)
    carry_ref[...] = jnp.zeros_like(carry_ref)

    def key_chunk(jc, diagonal):
        rows = pl.ds(pl.multiple_of(jc * SB_TILE, SB_TILE), SB_TILE)
        z = jnp.concatenate(
            [_dot_nt(q_ref[:, head_cols[h]], k_ref[rows, head_cols[h]]) for h in range(H_B)], axis=0)
        lp = jnp.log(1.0 + jnp.exp2(-jnp.abs(z))) * LOG2E
        log_beta = jnp.minimum(z, 0.0) - lp
        log_1m = log_beta - z
        if diagonal:
            causal = (lax.broadcasted_iota(jnp.int32, (stacked, SB_TILE), 1)
                      < (lax.broadcasted_iota(jnp.int32, (stacked, SB_TILE), 0) & (SB_TILE - 1)))
            log_1m = jnp.where(causal, log_1m, 0.0)
        hi = log_1m.astype(BF16)
        lo = (log_1m - hi.astype(F32)).astype(BF16)
        rest = _dot(jnp.concatenate([hi, lo], axis=1), suffix2)
        carry = carry_ref[...]
        a = jnp.exp2(log_beta + rest + jnp.concatenate([carry, carry], axis=1))
        if diagonal:
            a = jnp.where(causal, a, 0.0)
        a = a.astype(BF16)
        for h in range(H_B):
            acc_ref[head_rows[h], :] += _dot(a[head_rows[h], :], v_ref[rows, head_cols[h]])
        carry = carry + jnp.sum(log_1m, axis=-1, keepdims=True)
        carry_ref[...] = carry
        return (jnp.max(carry) > EXP2_UNDERFLOW).astype(jnp.int32)

    def cond(state):
        jc, live = state
        return jnp.logical_and(jc >= 0, live > 0)

    def body(state):
        jc, _ = state
        return jc - 1, key_chunk(jc, False)

    lax.while_loop(cond, body, (i - 1, key_chunk(i, True)))

    ssq = jnp.zeros((SB_TILE, 1), F32)
    for h in range(H_B):
        y = acc_ref[head_rows[h], :]
        ssq = ssq + jnp.sum(y * y, axis=-1, keepdims=True)
    inv = lax.rsqrt(ssq * (1.0 / w_b) + EPS)
    for h in range(H_B):
        o_ref[:, head_cols[h]] = (
            acc_ref[head_rows[h], :] * inv * gn_ref[:, head_cols[h]]).astype(o_ref.dtype)


def _sb(qkv, gn, *, bsz, seq, name):
    w_b = qkv.shape[1] // 3
    nq = seq // SB_TILE
    assert w_b == H_B * DH_B and seq % SB_TILE == 0
    return pl.pallas_call(
        _sb_kernel,
        grid=(bsz, nq),
        in_specs=[
            pl.BlockSpec((SB_TILE, w_b), lambda b, i: (b * nq + i, 0)),
            pl.BlockSpec((seq, w_b), lambda b, i: (b, 1)),
            pl.BlockSpec((seq, w_b), lambda b, i: (b, 2)),
            pl.BlockSpec((1, w_b), lambda b, i: (0, 0)),
        ],
        out_specs=pl.BlockSpec((SB_TILE, w_b), lambda b, i: (b * nq + i, 0)),
        out_shape=jax.ShapeDtypeStruct((bsz * seq, w_b), BF16),
        scratch_shapes=[pltpu.VMEM((H_B * SB_TILE, DH_B), F32),
                        pltpu.VMEM((H_B * SB_TILE, DH_B), F32)],
        compiler_params=_params(("parallel", "arbitrary"), 48),
        name=name,
    )(qkv, qkv, qkv, gn)


def _mixcross_kernel(h_ref, ya_ref, yb_ref, wmo_ref, cn_ref, wcq_ref, kv_ref, wco_ref, o_ref,
                     att_ref):
    d = h_ref.shape[1]
    w_a = ya_ref.shape[1]
    x_dh = d // X_HEADS
    h2 = h_ref[...] + _dot(ya_ref[...], wmo_ref[:w_a, :]) + _dot(yb_ref[...], wmo_ref[w_a:, :])
    xn = _rms(h2, cn_ref[...]).astype(BF16)
    q = (_dot(xn, wcq_ref[...]) * (x_dh ** -0.5)).astype(BF16)
    for hh in range(X_HEADS):
        cols = slice(hh * x_dh, (hh + 1) * x_dh)
        s = _dot_nt(q[:, cols], kv_ref[:, cols])
        p = jnp.exp(s - jnp.max(s, axis=-1, keepdims=True))
        p = p / jnp.sum(p, axis=-1, keepdims=True)
        oh = _dot(p.astype(BF16), kv_ref[:, d + hh * x_dh:d + (hh + 1) * x_dh])
        att_ref[:, cols] = oh.astype(BF16)
    o_ref[...] = h2 + _dot(att_ref[...], wco_ref[...])


def _mixcross(h, ya, yb, w_mo, cn, w_cq, kv, w_co, *, seq, tm=512, name):
    t, d = h.shape
    w_a = ya.shape[1]
    assert seq % tm == 0
    tiles_per_seq = seq // tm
    resident = lambda shape: pl.BlockSpec(shape, lambda i: (0, 0), pipeline_mode=pl.Buffered(1))
    return pl.pallas_call(
        _mixcross_kernel,
        grid=(t // tm,),
        in_specs=[
            pl.BlockSpec((tm, d), lambda i: (i, 0)),
            pl.BlockSpec((tm, w_a), lambda i: (i, 0)),
            pl.BlockSpec((tm, w_a), lambda i: (i, 0)),
            resident((d, d)),
            pl.BlockSpec((1, d), lambda i: (0, 0)),
            resident((d, d)),
            pl.BlockSpec((N_MEM, 2 * d), lambda i: (i // tiles_per_seq, 0)),
            resident((d, d)),
        ],
        out_specs=pl.BlockSpec((tm, d), lambda i: (i, 0)),
        out_shape=jax.ShapeDtypeStruct((t, d), F32),
        scratch_shapes=[pltpu.VMEM((tm, d), BF16)],
        compiler_params=_params(("parallel",), 60),
        name=name,
    )(h, ya, yb, w_mo, cn, w_cq, kv, w_co)


def kernel(x, mem, ffn1_norm, ffn1_w_in, ffn1_w_out, mix_norm, w_mix_in, ln_v_gain, ln_v_bias,
           spatial_w, spatial_b, gnorm_a, gnorm_b, w_mix_out, cross_norm, mem_norm, w_cq, w_ckv,
           w_co, ffn2_norm, ffn2_w_in, ffn2_w_out, final_norm):
    bsz, seq, d = x.shape
    depth = ffn1_norm.shape[0]
    assert depth >= 1
    w_a = gnorm_a.shape[1]
    w_b = gnorm_b.shape[1]
    h = x.reshape(bsz * seq, d)
    mem2 = mem.reshape(bsz * N_MEM, d)
    row = lambda v: v.reshape(1, -1)
    for l in range(depth):
        h = _ffn(h, row(ffn1_norm[l]), ffn1_w_in[l].astype(BF16), ffn1_w_out[l].astype(BF16),
                 name="ffn1")
        za, qkv = _mixin(h, row(mix_norm[l]), w_mix_in[l].astype(BF16), n_za=2 * w_a, name="mix_in")
        b_s_rows = jnp.broadcast_to(spatial_b[l].reshape(G_A * SGU_BLOCK, 1), (G_A * SGU_BLOCK, GA_DIM))
        ya = _sgu(za, row(ln_v_gain[l]), row(ln_v_bias[l]), spatial_w[l], b_s_rows,
                  row(gnorm_a[l]), name="sgu")
        yb = _sb(qkv, row(gnorm_b[l]), bsz=bsz, seq=seq, name="stickbreak")
        kv = _proj(mem2, row(mem_norm[l]), w_ckv[l].astype(BF16), out_dtype=BF16, name="mem_kv")
        h = _mixcross(h, ya, yb, w_mix_out[l].astype(BF16), row(cross_norm[l]),
                      w_cq[l].astype(BF16), kv, w_co[l].astype(BF16), seq=seq, name="mixcross")
        last = l == depth - 1
        h = _ffn(h, row(ffn2_norm[l]), ffn2_w_in[l].astype(BF16), ffn2_w_out[l].astype(BF16),
                 row(final_norm) if last else None, name="ffn2")
    return h.reshape(bsz, seq, d)
```

```python
import functools

import jax
import jax.numpy as jnp
from jax import lax
from jax.experimental import pallas as pl
from jax.experimental.pallas import tpu as pltpu

F32 = jnp.float32
BF16 = jnp.bfloat16

EPS = 1e-6
CHUNK = 64
N_MEM = 256
G_A = 8
GA_DIM = 128
SGU_BLOCK = 128
H_B = 8
DH_B = 128
SB_TILE = 256
LOG2E = 1.4426950408889634
EXP2_UNDERFLOW = -150.0
X_HEADS = 4

_MIB = 1024 * 1024
_NT_DIMS = (((1,), (1,)), ((), ()))


def _rms(x, g):
    return x * lax.rsqrt(jnp.mean(x * x, axis=-1, keepdims=True) + EPS) * g


def _dot(a, b):
    return jnp.dot(a, b, preferred_element_type=F32)


def _dot_nt(a, b):
    return lax.dot_general(a, b, _NT_DIMS, preferred_element_type=F32)


def _params(semantics, vmem_mib):
    return pltpu.CompilerParams(dimension_semantics=semantics,
                                vmem_limit_bytes=vmem_mib * _MIB)


def _ffn_kernel(x_ref, g_ref, wg_ref, wu_ref, wo_ref, *rest, final):
    if final:
        fg_ref, o_ref, xn_ref = rest
    else:
        o_ref, xn_ref = rest
    f = pl.program_id(1)

    @pl.when(f == 0)
    def _():
        x = x_ref[...]
        xn_ref[...] = _rms(x, g_ref[...]).astype(BF16)
        o_ref[...] = x

    xn = xn_ref[...]
    gate = _dot(xn, wg_ref[...])
    up = _dot(xn, wu_ref[...])
    a = (0.5 * gate) * (1.0 / (1.0 + jnp.exp(-gate))) * up
    o_ref[...] += _dot(a.astype(BF16), wo_ref[...])

    if final:
        @pl.when(f == pl.num_programs(1) - 1)
        def _():
            o_ref[...] = _rms(o_ref[...], fg_ref[...])


def _ffn(x, g, w_in, w_out, final_g=None, *, tm=1024, tf=512, name):
    t, d = x.shape
    d_ff = w_out.shape[0]
    nf = d_ff // tf
    assert t % tm == 0 and d_ff % tf == 0
    final = final_g is not None
    in_specs = [
        pl.BlockSpec((tm, d), lambda i, f: (i, 0)),
        pl.BlockSpec((1, d), lambda i, f: (0, 0)),
        pl.BlockSpec((d, tf), lambda i, f: (0, f)),
        pl.BlockSpec((d, tf), lambda i, f: (0, f + nf)),
        pl.BlockSpec((tf, d), lambda i, f: (f, 0)),
    ]
    args = [x, g, w_in, w_in, w_out]
    if final:
        in_specs.append(pl.BlockSpec((1, d), lambda i, f: (0, 0)))
        args.append(final_g)
    return pl.pallas_call(
        functools.partial(_ffn_kernel, final=final),
        grid=(t // tm, nf),
        in_specs=in_specs,
        out_specs=pl.BlockSpec((tm, d), lambda i, f: (i, 0)),
        out_shape=jax.ShapeDtypeStruct((t, d), F32),
        scratch_shapes=[pltpu.VMEM((tm, d), BF16)],
        compiler_params=_params(("parallel", "arbitrary"), 60),
        name=name,
    )(*args)


def _resident(shape):
    return pl.BlockSpec(shape, lambda i: (0,) * len(shape), pipeline_mode=pl.Buffered(1))


def _proj_kernel(x_ref, g_ref, w_ref, o_ref):
    xn = _rms(x_ref[...], g_ref[...]).astype(BF16)
    o_ref[...] = _dot(xn, w_ref[...]).astype(o_ref.dtype)


def _proj(x, g, w, *, out_dtype, tm=512, name):
    t, d = x.shape
    ncols = w.shape[1]
    assert t % tm == 0
    return pl.pallas_call(
        _proj_kernel,
        grid=(t // tm,),
        in_specs=[
            pl.BlockSpec((tm, d), lambda i: (i, 0)),
            pl.BlockSpec((1, d), lambda i: (0, 0)),
            _resident(w.shape),
        ],
        out_specs=pl.BlockSpec((tm, ncols), lambda i: (i, 0)),
        out_shape=jax.ShapeDtypeStruct((t, ncols), out_dtype),
        compiler_params=_params(("parallel",), 48),
        name=name,
    )(x, g, w)


def _gelu_tanh(z):
    c = 0.7978845608028654
    return 0.5 * z * (1.0 + jnp.tanh(c * (z + 0.044715 * (z * z * z))))


def _mixin_kernel(x_ref, g_ref, w_ref, lng_ref, lnb_ref, ws_ref, bs_ref, gn_ref, ya_ref, qkv_ref,
                  za_ref):
    n_za = za_ref.shape[1]
    w_b = qkv_ref.shape[1] // 3
    xn = _rms(x_ref[...], g_ref[...]).astype(BF16)
    za_ref[...] = _gelu_tanh(_dot(xn, w_ref[:, :n_za]))
    q = _dot(xn, w_ref[:, n_za:n_za + w_b]) * (LOG2E * DH_B ** -0.5)
    qkv_ref[:, :w_b] = q.astype(BF16)
    qkv_ref[:, w_b:] = _dot(xn, w_ref[:, n_za + w_b:]).astype(BF16)
    _sgu_tile(za_ref, lng_ref, lnb_ref, ws_ref, bs_ref, gn_ref, ya_ref)


def _mixin(x, g, w, ln_g, ln_b, w_s, b_s_rows, gn, *, tm=512, name):
    t, d = x.shape
    w_a = gn.shape[1]
    n_qkv = w.shape[1] - 2 * w_a
    assert t % tm == 0 and tm % SGU_BLOCK == 0 and w_a == G_A * GA_DIM
    const = lambda *shape: pl.BlockSpec(shape, lambda i: (0,) * len(shape))
    return pl.pallas_call(
        _mixin_kernel,
        grid=(t // tm,),
        in_specs=[
            pl.BlockSpec((tm, d), lambda i: (i, 0)),
            const(1, d),
            _resident(w.shape),
            const(1, w_a), const(1, w_a),
            const(G_A, SGU_BLOCK, SGU_BLOCK),
            const(G_A * SGU_BLOCK, GA_DIM),
            const(1, w_a),
        ],
        out_specs=[pl.BlockSpec((tm, w_a), lambda i: (i, 0)),
                   pl.BlockSpec((tm, n_qkv), lambda i: (i, 0))],
        out_shape=[jax.ShapeDtypeStruct((t, w_a), BF16),
                   jax.ShapeDtypeStruct((t, n_qkv), BF16)],
        scratch_shapes=[pltpu.VMEM((tm, 2 * w_a), F32)],
        compiler_params=_params(("parallel",), 56),
        name=name,
    )(x, g, w, ln_g, ln_b, w_s, b_s_rows, gn)


def _sgu_tile(za_ref, lng_ref, lnb_ref, ws_ref, bs_ref, gn_ref, o_ref):
    ts, w_a = o_ref.shape
    row_chunk = lax.broadcasted_iota(jnp.int32, (SGU_BLOCK, SGU_BLOCK), 0) // CHUNK
    col_chunk = lax.broadcasted_iota(jnp.int32, (SGU_BLOCK, SGU_BLOCK), 1) // CHUNK
    chunk_causal = col_chunk <= row_chunk
    w_masked = [jnp.where(chunk_causal, ws_ref[g], 0.0).astype(BF16) for g in range(G_A)]
    group_cols = [slice(g * GA_DIM, (g + 1) * GA_DIM) for g in range(G_A)]
    group_rows = [slice(g * SGU_BLOCK, (g + 1) * SGU_BLOCK) for g in range(G_A)]

    for blk in range(ts // SGU_BLOCK):
        rows = slice(blk * SGU_BLOCK, (blk + 1) * SGU_BLOCK)
        v = jnp.concatenate(
            [za_ref[rows, w_a + g * GA_DIM:w_a + (g + 1) * GA_DIM] for g in range(G_A)], axis=0)
        dv = v - jnp.mean(v, axis=-1, keepdims=True)
        dn = dv * lax.rsqrt(jnp.mean(dv * dv, axis=-1, keepdims=True) + EPS)
        ys = []
        y_sq = jnp.zeros((SGU_BLOCK, GA_DIM), F32)
        for g in range(G_A):
            vn = dn[group_rows[g], :] * lng_ref[:, group_cols[g]] + lnb_ref[:, group_cols[g]]
            mixed = _dot(w_masked[g], vn.astype(BF16)) + bs_ref[group_rows[g], :]
            y = za_ref[rows, group_cols[g]] * mixed
            y_sq = y_sq + y * y
            ys.append(y)
        inv = lax.rsqrt(jnp.sum(y_sq, axis=-1, keepdims=True) * (1.0 / w_a) + EPS)
        for g in range(G_A):
            o_ref[rows, group_cols[g]] = (ys[g] * inv * gn_ref[:, group_cols[g]]).astype(o_ref.dtype)


def _sb_kernel(q_ref, k_ref, v_ref, gn_ref, o_ref, acc_ref, carry_ref):
    i = pl.program_id(1)
    w_b = o_ref.shape[1]
    stacked = H_B * SB_TILE
    head_rows = [slice(h * SB_TILE, (h + 1) * SB_TILE) for h in range(H_B)]
    head_cols = [slice(h * DH_B, (h + 1) * DH_B) for h in range(H_B)]
    r = lax.broadcasted_iota(jnp.int32, (SB_TILE, SB_TILE), 0)
    c = lax.broadcasted_iota(jnp.int32, (SB_TILE, SB_TILE), 1)
    suffix = jnp.where(r > c, 1.0, 0.0).astype(BF16)
    suffix2 = jnp.concatenate([suffix, suffix], axis=0)

    acc_ref[...] = jnp.zeros_like(acc_ref)
    carry_ref[...] = jnp.zeros_like(carry_ref)

    def key_chunk(jc, diagonal):
        rows = pl.ds(pl.multiple_of(jc * SB_TILE, SB_TILE), SB_TILE)
        z = jnp.concatenate(
            [_dot_nt(q_ref[:, head_cols[h]], k_ref[rows, head_cols[h]]) for h in range(H_B)], axis=0)
        lp = jnp.log(1.0 + jnp.exp2(-jnp.abs(z))) * LOG2E
        log_beta = jnp.minimum(z, 0.0) - lp
        log_1m = log_beta - z
        if diagonal:
            causal = (lax.broadcasted_iota(jnp.int32, (stacked, SB_TILE), 1)
                      < (lax.broadcasted_iota(jnp.int32, (stacked, SB_TILE), 0) & (SB_TILE - 1)))
            log_1m = jnp.where(causal, log_1m, 0.0)
        hi = log_1m.astype(BF16)
        lo = (log_1m - hi.astype(F32)).astype(BF16)
        rest = _dot(jnp.concatenate([hi, lo], axis=1), suffix2)
        carry = carry_ref[...]
        a = jnp.exp2(log_beta + rest + jnp.concatenate([carry, carry], axis=1))
        if diagonal:
            a = jnp.where(causal, a, 0.0)
        a = a.astype(BF16)
        for h in range(H_B):
            acc_ref[head_rows[h], :] += _dot(a[head_rows[h], :], v_ref[rows, head_cols[h]])
        carry = carry + jnp.sum(log_1m, axis=-1, keepdims=True)
        carry_ref[...] = carry
        return (jnp.max(carry) > EXP2_UNDERFLOW).astype(jnp.int32)

    def cond(state):
        jc, live = state
        return jnp.logical_and(jc >= 0, live > 0)

    def body(state):
        jc, _ = state
        return jc - 1, key_chunk(jc, False)

    lax.while_loop(cond, body, (i - 1, key_chunk(i, True)))

    ssq = jnp.zeros((SB_TILE, 1), F32)
    for h in range(H_B):
        y = acc_ref[head_rows[h], :]
        ssq = ssq + jnp.sum(y * y, axis=-1, keepdims=True)
    inv = lax.rsqrt(ssq * (1.0 / w_b) + EPS)
    for h in range(H_B):
        o_ref[:, head_cols[h]] = (
            acc_ref[head_rows[h], :] * inv * gn_ref[:, head_cols[h]]).astype(o_ref.dtype)


def _sb(qkv, gn, *, bsz, seq, name):
    w_b = qkv.shape[1] // 3
    nq = seq // SB_TILE
    assert w_b == H_B * DH_B and seq % SB_TILE == 0
    return pl.pallas_call(
        _sb_kernel,
        grid=(bsz, nq),
        in_specs=[
            pl.BlockSpec((SB_TILE, w_b), lambda b, i: (b * nq + i, 0)),
            pl.BlockSpec((seq, w_b), lambda b, i: (b, 1)),
            pl.BlockSpec((seq, w_b), lambda b, i: (b, 2)),
            pl.BlockSpec((1, w_b), lambda b, i: (0, 0)),
        ],
        out_specs=pl.BlockSpec((SB_TILE, w_b), lambda b, i: (b * nq + i, 0)),
        out_shape=jax.ShapeDtypeStruct((bsz * seq, w_b), BF16),
        scratch_shapes=[pltpu.VMEM((H_B * SB_TILE, DH_B), F32),
                        pltpu.VMEM((H_B * SB_TILE, DH_B), F32)],
        compiler_params=_params(("parallel", "arbitrary"), 48),
        name=name,
    )(qkv, qkv, qkv, gn)


def _mixcross_kernel(h_ref, ya_ref, yb_ref, wmo_ref, cn_ref, wcq_ref, kv_ref, wco_ref, o_ref,
                     att_ref):
    d = h_ref.shape[1]
    w_a = ya_ref.shape[1]
    x_dh = d // X_HEADS
    h2 = h_ref[...] + _dot(ya_ref[...], wmo_ref[:w_a, :]) + _dot(yb_ref[...], wmo_ref[w_a:, :])
    xn = _rms(h2, cn_ref[...]).astype(BF16)
    q = (_dot(xn, wcq_ref[...]) * (x_dh ** -0.5)).astype(BF16)
    for hh in range(X_HEADS):
        cols = slice(hh * x_dh, (hh + 1) * x_dh)
        s = _dot_nt(q[:, cols], kv_ref[:, cols])
        p = jnp.exp(s - jnp.max(s, axis=-1, keepdims=True))
        p = p / jnp.sum(p, axis=-1, keepdims=True)
        oh = _dot(p.astype(BF16), kv_ref[:, d + hh * x_dh:d + (hh + 1) * x_dh])
        att_ref[:, cols] = oh.astype(BF16)
    o_ref[...] = h2 + _dot(att_ref[...], wco_ref[...])


def _mixcross(h, ya, yb, w_mo, cn, w_cq, kv, w_co, *, seq, tm=512, name):
    t, d = h.shape
    w_a = ya.shape[1]
    assert seq % tm == 0
    tiles_per_seq = seq // tm
    return pl.pallas_call(
        _mixcross_kernel,
        grid=(t // tm,),
        in_specs=[
            pl.BlockSpec((tm, d), lambda i: (i, 0)),
            pl.BlockSpec((tm, w_a), lambda i: (i, 0)),
            pl.BlockSpec((tm, w_a), lambda i: (i, 0)),
            _resident((d, d)),
            pl.BlockSpec((1, d), lambda i: (0, 0)),
            _resident((d, d)),
            pl.BlockSpec((N_MEM, 2 * d), lambda i: (i // tiles_per_seq, 0)),
            _resident((d, d)),
        ],
        out_specs=pl.BlockSpec((tm, d), lambda i: (i, 0)),
        out_shape=jax.ShapeDtypeStruct((t, d), F32),
        scratch_shapes=[pltpu.VMEM((tm, d), BF16)],
        compiler_params=_params(("parallel",), 60),
        name=name,
    )(h, ya, yb, w_mo, cn, w_cq, kv, w_co)


def kernel(x, mem, ffn1_norm, ffn1_w_in, ffn1_w_out, mix_norm, w_mix_in, ln_v_gain, ln_v_bias,
           spatial_w, spatial_b, gnorm_a, gnorm_b, w_mix_out, cross_norm, mem_norm, w_cq, w_ckv,
           w_co, ffn2_norm, ffn2_w_in, ffn2_w_out, final_norm):
    bsz, seq, d = x.shape
    depth = ffn1_norm.shape[0]
    assert depth >= 1
    w_a = gnorm_a.shape[1]
    w_b = gnorm_b.shape[1]
    h = x.reshape(bsz * seq, d)
    mem2 = mem.reshape(bsz * N_MEM, d)
    row = lambda v: v.reshape(1, -1)
    for l in range(depth):
        h = _ffn(h, row(ffn1_norm[l]), ffn1_w_in[l].astype(BF16), ffn1_w_out[l].astype(BF16),
                 name="ffn1")
        b_s_rows = jnp.broadcast_to(spatial_b[l].reshape(G_A * SGU_BLOCK, 1), (G_A * SGU_BLOCK, GA_DIM))
        ya, qkv = _mixin(h, row(mix_norm[l]), w_mix_in[l].astype(BF16), row(ln_v_gain[l]),
                         row(ln_v_bias[l]), spatial_w[l], b_s_rows, row(gnorm_a[l]), name="mix_in_sgu")
        yb = _sb(qkv, row(gnorm_b[l]), bsz=bsz, seq=seq, name="stickbreak")
        kv = _proj(mem2, row(mem_norm[l]), w_ckv[l].astype(BF16), out_dtype=BF16, name="mem_kv")
        h = _mixcross(h, ya, yb, w_mix_out[l].astype(BF16), row(cross_norm[l]),
                      w_cq[l].astype(BF16), kv, w_co[l].astype(BF16), seq=seq, name="mixcross")
        last = l == depth - 1
        h = _ffn(h, row(ffn2_norm[l]), ffn2_w_in[l].astype(BF16), ffn2_w_out[l].astype(BF16),
                 row(final_norm) if last else None, name="ffn2")
    return h.reshape(bsz, seq, d)
```

```python
import functools

import jax
import jax.numpy as jnp
from jax import lax
from jax.experimental import pallas as pl
from jax.experimental.pallas import tpu as pltpu

F32 = jnp.float32
BF16 = jnp.bfloat16

EPS = 1e-6
CHUNK = 64
N_MEM = 256
G_A = 8
GA_DIM = 128
SGU_BLOCK = 128
H_B = 8
DH_B = 128
SB_TILE = 256
LOG2E = 1.4426950408889634
EXP2_UNDERFLOW = -150.0
X_HEADS = 4

_MIB = 1024 * 1024
BF16_ROWS = 16
_NT_DIMS = (((1,), (1,)), ((), ()))


def _rms(x, g):
    return x * lax.rsqrt(jnp.mean(x * x, axis=-1, keepdims=True) + EPS) * g


def _dot(a, b):
    return jnp.dot(a, b, preferred_element_type=F32)


def _dot_nt(a, b):
    return lax.dot_general(a, b, _NT_DIMS, preferred_element_type=F32)


def _params(semantics, vmem_mib):
    return pltpu.CompilerParams(dimension_semantics=semantics,
                                vmem_limit_bytes=vmem_mib * _MIB)


def _ffn_kernel(x_ref, g_ref, wg_ref, wu_ref, wo_ref, *rest, final, n_cast):
    rest = list(rest)
    fg_ref = rest.pop(0) if final else None
    cast_src = [rest.pop(0) for _ in range(n_cast)]
    o_ref = rest.pop(0)
    cast_dst = [rest.pop(0) for _ in range(n_cast)]
    (xn_ref,) = rest
    f = pl.program_id(1)

    for src, dst in zip(cast_src, cast_dst):
        dst[...] = src[...].astype(BF16)

    @pl.when(f == 0)
    def _():
        x = x_ref[...]
        xn_ref[...] = _rms(x, g_ref[...]).astype(BF16)
        o_ref[...] = x

    xn = xn_ref[...]
    gate = _dot(xn, wg_ref[...])
    up = _dot(xn, wu_ref[...])
    a = (0.5 * gate) * (1.0 / (1.0 + jnp.exp(-gate))) * up
    o_ref[...] += _dot(a.astype(BF16), wo_ref[...])

    if final:
        @pl.when(f == pl.num_programs(1) - 1)
        def _():
            o_ref[...] = _rms(o_ref[...], fg_ref[...])


def _cast_row_block(rows, steps):
    rb = BF16_ROWS
    while rows % rb or rows // rb > steps:
        rb += BF16_ROWS
    return rb


def _ffn(x, g, w_in, w_out, final_g=None, *, side_casts=(), tm=1024, tf=512, name):
    t, d = x.shape
    d_ff = w_out.shape[0]
    nf = d_ff // tf
    assert t % tm == 0 and d_ff % tf == 0
    final = final_g is not None
    steps = (t // tm) * nf
    in_specs = [
        pl.BlockSpec((tm, d), lambda i, f: (i, 0)),
        pl.BlockSpec((1, d), lambda i, f: (0, 0)),
        pl.BlockSpec((d, tf), lambda i, f: (0, f)),
        pl.BlockSpec((d, tf), lambda i, f: (0, f + nf)),
        pl.BlockSpec((tf, d), lambda i, f: (f, 0)),
    ]
    args = [x, g, w_in, w_in, w_out]
    if final:
        in_specs.append(pl.BlockSpec((1, d), lambda i, f: (0, 0)))
        args.append(final_g)
    out_specs = [pl.BlockSpec((tm, d), lambda i, f: (i, 0))]
    out_shape = [jax.ShapeDtypeStruct((t, d), F32)]
    for w in side_casts:
        rows, cols = w.shape
        rb = _cast_row_block(rows, steps)
        spec = pl.BlockSpec(
            (rb, cols), lambda i, f, last=rows // rb - 1: (jnp.minimum(i * nf + f, last), 0))
        in_specs.append(spec)
        args.append(w)
        out_specs.append(spec)
        out_shape.append(jax.ShapeDtypeStruct((rows, cols), BF16))
    semantics = ("arbitrary", "arbitrary") if side_casts else ("parallel", "arbitrary")
    outs = pl.pallas_call(
        functools.partial(_ffn_kernel, final=final, n_cast=len(side_casts)),
        grid=(t // tm, nf),
        in_specs=in_specs,
        out_specs=out_specs,
        out_shape=out_shape,
        scratch_shapes=[pltpu.VMEM((tm, d), BF16)],
        compiler_params=_params(semantics, 60),
        name=name,
    )(*args)
    return outs if side_casts else outs[0]


def _resident(shape):
    return pl.BlockSpec(shape, lambda i: (0,) * len(shape), pipeline_mode=pl.Buffered(1))


def _proj_kernel(x_ref, g_ref, w_ref, o_ref):
    xn = _rms(x_ref[...], g_ref[...]).astype(BF16)
    o_ref[...] = _dot(xn, w_ref[...]).astype(o_ref.dtype)


def _proj(x, g, w, *, out_dtype, tm=512, name):
    t, d = x.shape
    ncols = w.shape[1]
    assert t % tm == 0
    return pl.pallas_call(
        _proj_kernel,
        grid=(t // tm,),
        in_specs=[
            pl.BlockSpec((tm, d), lambda i: (i, 0)),
            pl.BlockSpec((1, d), lambda i: (0, 0)),
            _resident(w.shape),
        ],
        out_specs=pl.BlockSpec((tm, ncols), lambda i: (i, 0)),
        out_shape=jax.ShapeDtypeStruct((t, ncols), out_dtype),
        compiler_params=_params(("parallel",), 48),
        name=name,
    )(x, g, w)


def _gelu_tanh(z):
    c = 0.7978845608028654
    return 0.5 * z * (1.0 + jnp.tanh(c * (z + 0.044715 * (z * z * z))))


def _mixin_kernel(x_ref, g_ref, w_ref, lng_ref, lnb_ref, ws_ref, bs_ref, gn_ref, ya_ref, qkv_ref,
                  za_ref):
    n_za = za_ref.shape[1]
    w_b = qkv_ref.shape[1] // 3
    xn = _rms(x_ref[...], g_ref[...]).astype(BF16)
    za_ref[...] = _gelu_tanh(_dot(xn, w_ref[:, :n_za]))
    q = _dot(xn, w_ref[:, n_za:n_za + w_b]) * (LOG2E * DH_B ** -0.5)
    qkv_ref[:, :w_b] = q.astype(BF16)
    qkv_ref[:, w_b:] = _dot(xn, w_ref[:, n_za + w_b:]).astype(BF16)
    _sgu_tile(za_ref, lng_ref, lnb_ref, ws_ref, bs_ref, gn_ref, ya_ref)


def _mixin(x, g, w, ln_g, ln_b, w_s, b_s_rows, gn, *, tm=512, name):
    t, d = x.shape
    w_a = gn.shape[1]
    n_qkv = w.shape[1] - 2 * w_a
    assert t % tm == 0 and tm % SGU_BLOCK == 0 and w_a == G_A * GA_DIM
    const = lambda *shape: pl.BlockSpec(shape, lambda i: (0,) * len(shape))
    return pl.pallas_call(
        _mixin_kernel,
        grid=(t // tm,),
        in_specs=[
            pl.BlockSpec((tm, d), lambda i: (i, 0)),
            const(1, d),
            _resident(w.shape),
            const(1, w_a), const(1, w_a),
            const(G_A, SGU_BLOCK, SGU_BLOCK),
            const(G_A * SGU_BLOCK, GA_DIM),
            const(1, w_a),
        ],
        out_specs=[pl.BlockSpec((tm, w_a), lambda i: (i, 0)),
                   pl.BlockSpec((tm, n_qkv), lambda i: (i, 0))],
        out_shape=[jax.ShapeDtypeStruct((t, w_a), BF16),
                   jax.ShapeDtypeStruct((t, n_qkv), BF16)],
        scratch_shapes=[pltpu.VMEM((tm, 2 * w_a), F32)],
        compiler_params=_params(("parallel",), 56),
        name=name,
    )(x, g, w, ln_g, ln_b, w_s, b_s_rows, gn)


def _sgu_tile(za_ref, lng_ref, lnb_ref, ws_ref, bs_ref, gn_ref, o_ref):
    ts, w_a = o_ref.shape
    row_chunk = lax.broadcasted_iota(jnp.int32, (SGU_BLOCK, SGU_BLOCK), 0) // CHUNK
    col_chunk = lax.broadcasted_iota(jnp.int32, (SGU_BLOCK, SGU_BLOCK), 1) // CHUNK
    chunk_causal = col_chunk <= row_chunk
    w_masked = [jnp.where(chunk_causal, ws_ref[g], 0.0).astype(BF16) for g in range(G_A)]
    group_cols = [slice(g * GA_DIM, (g + 1) * GA_DIM) for g in range(G_A)]
    group_rows = [slice(g * SGU_BLOCK, (g + 1) * SGU_BLOCK) for g in range(G_A)]

    for blk in range(ts // SGU_BLOCK):
        rows = slice(blk * SGU_BLOCK, (blk + 1) * SGU_BLOCK)
        v = jnp.concatenate(
            [za_ref[rows, w_a + g * GA_DIM:w_a + (g + 1) * GA_DIM] for g in range(G_A)], axis=0)
        dv = v - jnp.mean(v, axis=-1, keepdims=True)
        dn = dv * lax.rsqrt(jnp.mean(dv * dv, axis=-1, keepdims=True) + EPS)
        ys = []
        y_sq = jnp.zeros((SGU_BLOCK, GA_DIM), F32)
        for g in range(G_A):
            vn = dn[group_rows[g], :] * lng_ref[:, group_cols[g]] + lnb_ref[:, group_cols[g]]
            mixed = _dot(w_masked[g], vn.astype(BF16)) + bs_ref[group_rows[g], :]
            y = za_ref[rows, group_cols[g]] * mixed
            y_sq = y_sq + y * y
            ys.append(y)
        inv = lax.rsqrt(jnp.sum(y_sq, axis=-1, keepdims=True) * (1.0 / w_a) + EPS)
        for g in range(G_A):
            o_ref[rows, group_cols[g]] = (ys[g] * inv * gn_ref[:, group_cols[g]]).astype(o_ref.dtype)


def _sb_kernel(q_ref, k_ref, v_ref, gn_ref, o_ref, acc_ref, carry_ref):
    i = pl.program_id(1)
    w_b = o_ref.shape[1]
    stacked = H_B * SB_TILE
    head_rows = [slice(h * SB_TILE, (h + 1) * SB_TILE) for h in range(H_B)]
    head_cols = [slice(h * DH_B, (h + 1) * DH_B) for h in range(H_B)]
    r = lax.broadcasted_iota(jnp.int32, (SB_TILE, SB_TILE), 0)
    c = lax.broadcasted_iota(jnp.int32, (SB_TILE, SB_TILE), 1)
    suffix = jnp.where(r > c, 1.0, 0.0).astype(BF16)
    suffix2 = jnp.concatenate([suffix, suffix], axis=0)

    acc_ref[...] = jnp.zeros_like(acc_ref)
    carry_ref[...] = jnp.zeros_like(carry_ref)

    def key_chunk(jc, diagonal):
        rows = pl.ds(pl.multiple_of(jc * SB_TILE, SB_TILE), SB_TILE)
        z = jnp.concatenate(
            [_dot_nt(q_ref[:, head_cols[h]], k_ref[rows, head_cols[h]]) for h in range(H_B)], axis=0)
        lp = jnp.log(1.0 + jnp.exp2(-jnp.abs(z))) * LOG2E
        log_beta = jnp.minimum(z, 0.0) - lp
        log_1m = log_beta - z
        if diagonal:
            causal = (lax.broadcasted_iota(jnp.int32, (stacked, SB_TILE), 1)
                      < (lax.broadcasted_iota(jnp.int32, (stacked, SB_TILE), 0) & (SB_TILE - 1)))
            log_1m = jnp.where(causal, log_1m, 0.0)
        hi = log_1m.astype(BF16)
        lo = (log_1m - hi.astype(F32)).astype(BF16)
        rest = _dot(jnp.concatenate([hi, lo], axis=1), suffix2)
        carry = carry_ref[...]
        a = jnp.exp2(log_beta + rest + jnp.concatenate([carry, carry], axis=1))
        if diagonal:
            a = jnp.where(causal, a, 0.0)
        a = a.astype(BF16)
        for h in range(H_B):
            acc_ref[head_rows[h], :] += _dot(a[head_rows[h], :], v_ref[rows, head_cols[h]])
        carry = carry + jnp.sum(log_1m, axis=-1, keepdims=True)
        carry_ref[...] = carry
        return (jnp.max(carry) > EXP2_UNDERFLOW).astype(jnp.int32)

    def cond(state):
        jc, live = state
        return jnp.logical_and(jc >= 0, live > 0)

    def body(state):
        jc, _ = state
        return jc - 1, key_chunk(jc, False)

    lax.while_loop(cond, body, (i - 1, key_chunk(i, True)))

    ssq = jnp.zeros((SB_TILE, 1), F32)
    for h in range(H_B):
        y = acc_ref[head_rows[h], :]
        ssq = ssq + jnp.sum(y * y, axis=-1, keepdims=True)
    inv = lax.rsqrt(ssq * (1.0 / w_b) + EPS)
    for h in range(H_B):
        o_ref[:, head_cols[h]] = (
            acc_ref[head_rows[h], :] * inv * gn_ref[:, head_cols[h]]).astype(o_ref.dtype)


def _sb(qkv, gn, *, bsz, seq, name):
    w_b = qkv.shape[1] // 3
    nq = seq // SB_TILE
    assert w_b == H_B * DH_B and seq % SB_TILE == 0
    return pl.pallas_call(
        _sb_kernel,
        grid=(bsz, nq),
        in_specs=[
            pl.BlockSpec((SB_TILE, w_b), lambda b, i: (b * nq + i, 0)),
            pl.BlockSpec((seq, w_b), lambda b, i: (b, 1)),
            pl.BlockSpec((seq, w_b), lambda b, i: (b, 2)),
            pl.BlockSpec((1, w_b), lambda b, i: (0, 0)),
        ],
        out_specs=pl.BlockSpec((SB_TILE, w_b), lambda b, i: (b * nq + i, 0)),
        out_shape=jax.ShapeDtypeStruct((bsz * seq, w_b), BF16),
        scratch_shapes=[pltpu.VMEM((H_B * SB_TILE, DH_B), F32),
                        pltpu.VMEM((H_B * SB_TILE, DH_B), F32)],
        compiler_params=_params(("parallel", "arbitrary"), 48),
        name=name,
    )(qkv, qkv, qkv, gn)


def _mixcross_kernel(h_ref, ya_ref, yb_ref, wmo_ref, cn_ref, wcq_ref, kv_ref, wco_ref, o_ref,
                     att_ref):
    d = h_ref.shape[1]
    w_a = ya_ref.shape[1]
    x_dh = d // X_HEADS
    h2 = h_ref[...] + _dot(ya_ref[...], wmo_ref[:w_a, :]) + _dot(yb_ref[...], wmo_ref[w_a:, :])
    xn = _rms(h2, cn_ref[...]).astype(BF16)
    q = (_dot(xn, wcq_ref[...]) * (x_dh ** -0.5)).astype(BF16)
    for hh in range(X_HEADS):
        cols = slice(hh * x_dh, (hh + 1) * x_dh)
        s = _dot_nt(q[:, cols], kv_ref[:, cols])
        p = jnp.exp(s - jnp.max(s, axis=-1, keepdims=True))
        p = p / jnp.sum(p, axis=-1, keepdims=True)
        oh = _dot(p.astype(BF16), kv_ref[:, d + hh * x_dh:d + (hh + 1) * x_dh])
        att_ref[:, cols] = oh.astype(BF16)
    o_ref[...] = h2 + _dot(att_ref[...], wco_ref[...])


def _mixcross(h, ya, yb, w_mo, cn, w_cq, kv, w_co, *, seq, tm=512, name):
    t, d = h.shape
    w_a = ya.shape[1]
    assert seq % tm == 0
    tiles_per_seq = seq // tm
    return pl.pallas_call(
        _mixcross_kernel,
        grid=(t // tm,),
        in_specs=[
            pl.BlockSpec((tm, d), lambda i: (i, 0)),
            pl.BlockSpec((tm, w_a), lambda i: (i, 0)),
            pl.BlockSpec((tm, w_a), lambda i: (i, 0)),
            _resident((d, d)),
            pl.BlockSpec((1, d), lambda i: (0, 0)),
            _resident((d, d)),
            pl.BlockSpec((N_MEM, 2 * d), lambda i: (i // tiles_per_seq, 0)),
            _resident((d, d)),
        ],
        out_specs=pl.BlockSpec((tm, d), lambda i: (i, 0)),
        out_shape=jax.ShapeDtypeStruct((t, d), F32),
        scratch_shapes=[pltpu.VMEM((tm, d), BF16)],
        compiler_params=_params(("parallel",), 60),
        name=name,
    )(h, ya, yb, w_mo, cn, w_cq, kv, w_co)


def kernel(x, mem, ffn1_norm, ffn1_w_in, ffn1_w_out, mix_norm, w_mix_in, ln_v_gain, ln_v_bias,
           spatial_w, spatial_b, gnorm_a, gnorm_b, w_mix_out, cross_norm, mem_norm, w_cq, w_ckv,
           w_co, ffn2_norm, ffn2_w_in, ffn2_w_out, final_norm):
    bsz, seq, d = x.shape
    depth = ffn1_norm.shape[0]
    assert depth >= 1
    w_a = gnorm_a.shape[1]
    w_b = gnorm_b.shape[1]
    h = x.reshape(bsz * seq, d)
    mem2 = mem.reshape(bsz * N_MEM, d)
    row = lambda v: v.reshape(1, -1)
    for l in range(depth):
        later = (w_mix_in[l], w_mix_out[l], w_cq[l], w_ckv[l], w_co[l], ffn2_w_in[l], ffn2_w_out[l])
        h, wb_mix_in, wb_mix_out, wb_cq, wb_ckv, wb_co, wb_ffn2_in, wb_ffn2_out = _ffn(
            h, row(ffn1_norm[l]), ffn1_w_in[l].astype(BF16), ffn1_w_out[l].astype(BF16),
            side_casts=later, name="ffn1")
        b_s_rows = jnp.broadcast_to(spatial_b[l].reshape(G_A * SGU_BLOCK, 1), (G_A * SGU_BLOCK, GA_DIM))
        ya, qkv = _mixin(h, row(mix_norm[l]), wb_mix_in, row(ln_v_gain[l]),
                         row(ln_v_bias[l]), spatial_w[l], b_s_rows, row(gnorm_a[l]), name="mix_in_sgu")
        yb = _sb(qkv, row(gnorm_b[l]), bsz=bsz, seq=seq, name="stickbreak")
        kv = _proj(mem2, row(mem_norm[l]), wb_ckv, out_dtype=BF16, name="mem_kv")
        h = _mixcross(h, ya, yb, wb_mix_out, row(cross_norm[l]), wb_cq, kv, wb_co, seq=seq,
                      name="mixcross")
        last = l == depth - 1
        h = _ffn(h, row(ffn2_norm[l]), wb_ffn2_in, wb_ffn2_out,
                 row(final_norm) if last else None, name="ffn2")
    return h.reshape(bsz, seq, d)
```
